```python
import math
import jax, jax.numpy as jnp
from jax import lax
import numpy as np

D_MODEL = 1024
BATCH = 4
SEQ = 8192
DEPTH = 4

N_MEM = 256
A_HEADS = 8
A_HEAD_DIM = 64
A_WIDTH = A_HEADS * A_HEAD_DIM
KV_RANK = 128
IDX_HEADS = 8
IDX_DIM = 32
TOPK_MAX = 256
Q_BLOCK = 128
POOL_WINDOWS = (2, 4, 8, 16)
N_POOL_GROUPS = 4
POOL_GROUP = 64
POOL_WIDTH = N_POOL_GROUPS * POOL_GROUP
C_HEADS = 4
C_HEAD_DIM = 64
C_WIDTH = C_HEADS * C_HEAD_DIM
N_BRANCH = 3
REL_BUCKETS = 32
REL_MAX_DIST = 128
D_FF = 2816
EPS = 1e-6
NEG = -1e30

IN_SPLITS = (A_WIDTH, KV_RANK, IDX_HEADS * IDX_DIM, IDX_DIM, IDX_HEADS,
             POOL_WIDTH, C_WIDTH, N_BRANCH * D_MODEL)
IN_WIDTH = sum(IN_SPLITS)

kernel_name = "hybrid_dsa_pool_memattn_macaron"


def _split_points(sizes):
    pts, acc = [], 0
    for s in sizes[:-1]:
        acc += s
        pts.append(acc)
    return pts


def rmsnorm(x, g):
    xf = x.astype(jnp.float32)
    y = xf * lax.rsqrt(jnp.mean(xf * xf, axis=-1, keepdims=True) + EPS)
    return (y * g.astype(jnp.float32)).astype(x.dtype)


def swiglu(x, w_in, w_out):
    a, b = jnp.split(x @ w_in, 2, axis=-1)
    return (jax.nn.silu(a) * b) @ w_out


def t5_bucket(dist):
    max_exact = REL_BUCKETS // 2
    n = jnp.maximum(dist, 0)
    nf = jnp.maximum(n, 1).astype(jnp.float32)
    large = max_exact + (jnp.log(nf / max_exact) / math.log(REL_MAX_DIST / max_exact)
                         * (REL_BUCKETS - max_exact)).astype(jnp.int32)
    large = jnp.minimum(large, REL_BUCKETS - 1)
    return jnp.where(n < max_exact, n, large)


def dsa_attention(q, c_kv, iq, ik, iw, w_uk, w_uv, rel_bias):
    B, S = q.shape[0], q.shape[1]
    k_sel = min(TOPK_MAX, S // 4)
    n_blk = S // Q_BLOCK
    q_lat = jnp.einsum('bshd,rhd->bshr', q, w_uk)
    key_pos = jnp.arange(S)
    scale = A_HEAD_DIM ** -0.5
    idx_scale = IDX_DIM ** -0.5
    w_scale = IDX_HEADS ** -0.5

    def block(i):
        start = i * Q_BLOCK
        qp = start + jnp.arange(Q_BLOCK)
        iq_b = lax.dynamic_slice_in_dim(iq, start, Q_BLOCK, axis=1)
        iw_b = lax.dynamic_slice_in_dim(iw, start, Q_BLOCK, axis=1)
        ql_b = lax.dynamic_slice_in_dim(q_lat, start, Q_BLOCK, axis=1)
        dots = jax.nn.relu(jnp.einsum('bqhd,bkd->bqhk', iq_b, ik) * idx_scale)
        score = jnp.einsum('bqh,bqhk->bqk', iw_b * w_scale, dots).astype(jnp.float32)
        causal = key_pos[None, :] <= qp[:, None]
        score = jnp.where(causal[None], score, -jnp.inf)
        _, idx = lax.top_k(score, k_sel)
        valid = idx <= qp[None, :, None]
        c_sel = jax.vmap(lambda c, j: c[j])(c_kv, idx)
        logits = jnp.einsum('bqhr,bqkr->bqhk', ql_b, c_sel).astype(jnp.float32) * scale
        bias = rel_bias[t5_bucket(qp[None, :, None] - idx)]
        logits = logits + jnp.transpose(bias, (0, 1, 3, 2)).astype(jnp.float32)
        logits = jnp.where(valid[:, :, None, :], logits, NEG)
        p = jax.nn.softmax(logits, axis=-1).astype(c_sel.dtype)
        o_lat = jnp.einsum('bqhk,bqkr->bqhr', p, c_sel)
        return jnp.einsum('bqhr,rhd->bqhd', o_lat, w_uv)

    out = lax.map(block, jnp.arange(n_blk))
    return jnp.moveaxis(out, 0, 1).reshape(B, S, A_WIDTH)


def pool_mixer(z, w_pool, pool_scale):
    B, S, C = z.shape
    zf = z.astype(jnp.float32)
    cs = jnp.cumsum(zf, axis=1)
    wmax = max(POOL_WINDOWS)
    cs_pad = jnp.pad(cs, ((0, 0), (wmax, 0), (0, 0)))
    pos = jnp.arange(1, S + 1, dtype=jnp.float32)
    outs = []
    for g, w in enumerate(POOL_WINDOWS):
        lo, hi = g * POOL_GROUP, (g + 1) * POOL_GROUP
        win_sum = cs[:, :, lo:hi] - cs_pad[:, wmax - w: wmax - w + S, lo:hi]
        count = jnp.minimum(pos, float(w))[None, :, None]
        pooled = win_sum / count - zf[:, :, lo:hi]
        outs.append(jnp.einsum('bsc,cd->bsd', pooled.astype(z.dtype), w_pool[g]))
    return jnp.concatenate(outs, axis=-1) * pool_scale


def mem_attention(qc, mem_n, w_mem_kv):
    B, S = qc.shape[0], qc.shape[1]
    M = mem_n.shape[1]
    k, v = jnp.split(mem_n @ w_mem_kv, 2, axis=-1)
    q = qc.reshape(B, S, C_HEADS, C_HEAD_DIM)
    k = k.reshape(B, M, C_HEADS, C_HEAD_DIM)
    v = v.reshape(B, M, C_HEADS, C_HEAD_DIM)
    logits = jnp.einsum('bshd,bmhd->bhsm', q, k).astype(jnp.float32) * C_HEAD_DIM ** -0.5
    p = jax.nn.softmax(logits, axis=-1).astype(v.dtype)
    return jnp.einsum('bhsm,bmhd->bshd', p, v).reshape(B, S, C_WIDTH)


def setup_inputs(seed: int = 0) -> dict:
    key = jax.random.key(seed)
    ks = jax.random.split(key, 24)
    f32 = jnp.float32
    L = DEPTH

    def nrm(k, shape, fan_in):
        return jax.random.normal(k, shape, f32) * fan_in ** -0.5

    def gain(k, shape):
        return 1.0 + 0.05 * jax.random.normal(k, shape, f32)

    return {
        "x": jax.random.normal(ks[0], (BATCH, SEQ, D_MODEL), f32),
        "mem": jax.random.normal(ks[1], (BATCH, N_MEM, D_MODEL), f32),
        "rel_bias": 0.5 * jax.random.normal(ks[2], (REL_BUCKETS, A_HEADS), f32),
        "ffn1_norm": gain(ks[3], (L, D_MODEL)),
        "ffn1_w_in": nrm(ks[4], (L, D_MODEL, 2 * D_FF), D_MODEL),
        "ffn1_w_out": nrm(ks[5], (L, D_FF, D_MODEL), D_FF),
        "mix_norm": gain(ks[6], (L, D_MODEL)),
        "w_in": nrm(ks[7], (L, D_MODEL, IN_WIDTH), D_MODEL),
        "kv_norm": gain(ks[8], (L, KV_RANK)),
        "w_uk": nrm(ks[9], (L, KV_RANK, A_HEADS, A_HEAD_DIM), KV_RANK),
        "w_uv": nrm(ks[10], (L, KV_RANK, A_HEADS, A_HEAD_DIM), KV_RANK),
        "w_pool": nrm(ks[11], (L, N_POOL_GROUPS, POOL_GROUP, POOL_GROUP), POOL_GROUP),
        "pool_scale": gain(ks[12], (L, POOL_WIDTH)),
        "mem_norm": gain(ks[13], (L, D_MODEL)),
        "w_mem_kv": nrm(ks[14], (L, D_MODEL, 2 * C_WIDTH), D_MODEL),
        "w_branch_a": nrm(ks[15], (L, A_WIDTH, D_MODEL), A_WIDTH),
        "w_branch_b": nrm(ks[16], (L, POOL_WIDTH, D_MODEL), POOL_WIDTH),
        "w_branch_c": nrm(ks[17], (L, C_WIDTH, D_MODEL), C_WIDTH),
        "w_out": nrm(ks[18], (L, D_MODEL, D_MODEL), D_MODEL),
        "ffn2_norm": gain(ks[19], (L, D_MODEL)),
        "ffn2_w_in": nrm(ks[20], (L, D_MODEL, 2 * D_FF), D_MODEL),
        "ffn2_w_out": nrm(ks[21], (L, D_FF, D_MODEL), D_FF),
        "final_norm": gain(ks[22], (D_MODEL,)),
    }


def reference(x, mem, rel_bias, ffn1_norm, ffn1_w_in, ffn1_w_out, mix_norm, w_in, kv_norm,
              w_uk, w_uv, w_pool, pool_scale, mem_norm, w_mem_kv, w_branch_a, w_branch_b,
              w_branch_c, w_out, ffn2_norm, ffn2_w_in, ffn2_w_out, final_norm):
    B, S, D = x.shape
    split_pts = _split_points(IN_SPLITS)
    h = x
    for l in range(DEPTH):
        h = h + 0.5 * swiglu(rmsnorm(h, ffn1_norm[l]), ffn1_w_in[l], ffn1_w_out[l])

        u = rmsnorm(h, mix_norm[l])
        q_a, c_kv, iq, ik, iw, z_pool, q_c, gate_pre = jnp.split(u @ w_in[l], split_pts, axis=-1)
        q_a = q_a.reshape(B, S, A_HEADS, A_HEAD_DIM)
        c_kv = rmsnorm(c_kv, kv_norm[l])
        iq = iq.reshape(B, S, IDX_HEADS, IDX_DIM)

        y_a = dsa_attention(q_a, c_kv, iq, ik, iw, w_uk[l], w_uv[l], rel_bias)
        y_b = pool_mixer(z_pool, w_pool[l], pool_scale[l])
        y_c = mem_attention(q_c, rmsnorm(mem, mem_norm[l]), w_mem_kv[l])

        g = jax.nn.sigmoid(gate_pre).reshape(B, S, N_BRANCH, D)
        merged = (g[:, :, 0] * (y_a @ w_branch_a[l])
                  + g[:, :, 1] * (y_b @ w_branch_b[l])
                  + g[:, :, 2] * (y_c @ w_branch_c[l]))
        h = h + merged @ w_out[l]

        h = h + 0.5 * swiglu(rmsnorm(h, ffn2_norm[l]), ffn2_w_in[l], ffn2_w_out[l])
    return rmsnorm(h, final_norm)
```

```python
import functools
import math

import numpy as np
import jax
import jax.numpy as jnp
from jax import lax
from jax.experimental import pallas as pl
from jax.experimental.pallas import tpu as pltpu

F32 = jnp.float32
BF16 = jnp.bfloat16

A_HEADS = 8
A_HEAD_DIM = 64
A_WIDTH = A_HEADS * A_HEAD_DIM
KV_RANK = 128
IDX_HEADS = 8
IDX_DIM = 32
TOPK_MAX = 256
Q_BLOCK = 128
POOL_WINDOWS = (2, 4, 8, 16)
POOL_GROUP = 64
POOL_WIDTH = len(POOL_WINDOWS) * POOL_GROUP
POOL_HALO = max(POOL_WINDOWS)
C_HEADS = 4
C_HEAD_DIM = 64
C_WIDTH = C_HEADS * C_HEAD_DIM
N_BRANCH = 3
REL_BUCKETS = 32
REL_MAX_DIST = 128
EPS = 1e-6
NEG = -1e30
M_INIT = -1e29

LANES = 128
VMEM_LIMIT = 52 * 1024 * 1024
FFN_TM = 512
PROJ_TM = 512
MIX_TM = 512
IDX_TK = 512
ATT_TK = 512
NEAR_W = 2 * Q_BLOCK

INT_MIN = -2 ** 31
KEY_NEG_INF = -2139095040
KEY_POS_INF = 2139095040


def _rms(x, g):
    return x * lax.rsqrt(jnp.mean(x * x, axis=-1, keepdims=True) + EPS) * g


def _params(*sem):
    return pltpu.CompilerParams(dimension_semantics=sem, vmem_limit_bytes=VMEM_LIMIT)


def _ffn_kernel(x_ref, g_ref, wa_ref, wb_ref, wo_ref, fg_ref, o_ref, xn_ref, acc_ref, *, n_ff, final):
    j = pl.program_id(1)

    @pl.when(j == 0)
    def _():
        xn_ref[...] = _rms(x_ref[...], g_ref[...]).astype(BF16)
        acc_ref[...] = jnp.zeros_like(acc_ref)

    xn = xn_ref[...]
    a = jnp.dot(xn, wa_ref[...], preferred_element_type=F32)
    b = jnp.dot(xn, wb_ref[...], preferred_element_type=F32)
    act = (a * jax.nn.sigmoid(a) * b).astype(BF16)
    acc_ref[...] += jnp.dot(act, wo_ref[...], preferred_element_type=F32)

    @pl.when(j == n_ff - 1)
    def _():
        y = x_ref[...] + 0.5 * acc_ref[...]
        if final:
            y = _rms(y, fg_ref[...])
        o_ref[...] = y


def _ffn(h, g, w_in, w_out, fg, layer, *, final):
    T, D = h.shape
    F = w_out.shape[1]
    n_ff = 2 if F % (2 * LANES) == 0 else 1
    fc = F // n_ff
    tm = min(FFN_TM, T)
    return pl.pallas_call(
        functools.partial(_ffn_kernel, n_ff=n_ff, final=final),
        out_shape=jax.ShapeDtypeStruct((T, D), F32),
        grid=(T // tm, n_ff),
        in_specs=[
            pl.BlockSpec((tm, D), lambda i, j: (i, 0)),
            pl.BlockSpec((None, 1, D), lambda i, j: (layer, 0, 0)),
            pl.BlockSpec((None, D, fc), lambda i, j: (layer, 0, j)),
            pl.BlockSpec((None, D, fc), lambda i, j: (layer, 0, n_ff + j)),
            pl.BlockSpec((None, fc, D), lambda i, j: (layer, j, 0)),
            pl.BlockSpec((1, D), lambda i, j: (0, 0)),
        ],
        out_specs=pl.BlockSpec((tm, D), lambda i, j: (i, 0)),
        scratch_shapes=[pltpu.VMEM((tm, D), BF16), pltpu.VMEM((tm, D), F32)],
        compiler_params=_params("parallel", "arbitrary"),
        name="ffn",
    )(h, g, w_in, w_in, w_out, fg)


def _proj_kernel(h_ref, g_ref, wq_ref, wuk_ref, wckv_ref, kvg_ref, wiq_ref, wikt_ref, wiw_ref, wz_ref, wqc_ref,
                 ql_ref, ckv_ref, iq_ref, ikt_ref, iw_ref, z_ref, qc_ref):
    u = _rms(h_ref[...], g_ref[...]).astype(BF16)
    q = jnp.dot(u, wq_ref[...], preferred_element_type=F32).astype(BF16)
    ql = jnp.dot(q, wuk_ref[...], preferred_element_type=F32) * (A_HEAD_DIM ** -0.5)
    ql_ref[...] = ql.astype(BF16)
    c = jnp.dot(u, wckv_ref[...], preferred_element_type=F32)
    ckv_ref[...] = _rms(c, kvg_ref[...]).astype(BF16)
    iq_ref[...] = jnp.dot(u, wiq_ref[...], preferred_element_type=F32).astype(BF16)
    ikt = lax.dot_general(wikt_ref[...], u, (((1,), (1,)), ((), ())), preferred_element_type=F32)
    ikt_ref[...] = ikt.astype(BF16)
    iw = jnp.dot(u, wiw_ref[...], preferred_element_type=F32)
    iw_ref[...] = iw[:, :IDX_HEADS] * ((IDX_DIM ** -0.5) * (IDX_HEADS ** -0.5))
    z_ref[...] = jnp.dot(u, wz_ref[...], preferred_element_type=F32)
    qc = jnp.dot(u, wqc_ref[...], preferred_element_type=F32) * (C_HEAD_DIM ** -0.5)
    qc_ref[...] = qc.astype(BF16)


def _proj(h, g, wp, layer, B, S):
    T, D = h.shape
    tm = min(PROJ_TM, S)
    nt = S // tm
    RL = A_HEADS * KV_RANK

    def wspec(arr):
        return pl.BlockSpec((None,) + arr.shape[1:], lambda b, i: (layer,) + (0,) * (arr.ndim - 1))

    tok = lambda w: pl.BlockSpec((tm, w), lambda b, i: (b * nt + i, 0))
    outs = pl.pallas_call(
        _proj_kernel,
        out_shape=[
            jax.ShapeDtypeStruct((T, RL), BF16),
            jax.ShapeDtypeStruct((T, KV_RANK), BF16),
            jax.ShapeDtypeStruct((T, IDX_HEADS * IDX_DIM), BF16),
            jax.ShapeDtypeStruct((B, IDX_DIM, S), BF16),
            jax.ShapeDtypeStruct((T, IDX_HEADS), F32),
            jax.ShapeDtypeStruct((T, POOL_WIDTH), F32),
            jax.ShapeDtypeStruct((T, C_WIDTH), BF16),
        ],
        grid=(B, nt),
        in_specs=[tok(D), wspec(g)] + [wspec(wp[k]) for k in
                                        ("wq", "wuk", "wckv", "kvg", "wiq", "wikt", "wiw", "wz", "wqc")],
        out_specs=[tok(RL), tok(KV_RANK), tok(IDX_HEADS * IDX_DIM),
                   pl.BlockSpec((None, IDX_DIM, tm), lambda b, i: (b, 0, i)),
                   tok(IDX_HEADS), tok(POOL_WIDTH), tok(C_WIDTH)],
        compiler_params=_params("parallel", "parallel"),
        name="proj",
    )(h, g, *[wp[k] for k in ("wq", "wuk", "wckv", "kvg", "wiq", "wikt", "wiw", "wz", "wqc")])
    return outs


def _t5_bucket_np(dist):
    max_exact = REL_BUCKETS // 2
    n = np.maximum(dist, 0)
    nf = np.maximum(n, 1).astype(np.float32)
    large = max_exact + (np.log(nf / np.float32(max_exact)) / np.float32(math.log(REL_MAX_DIST / max_exact))
                         * np.float32(REL_BUCKETS - max_exact)).astype(np.int32)
    large = np.minimum(large, REL_BUCKETS - 1)
    return np.where(n < max_exact, n, large).astype(np.int32)


def _bias_kernel(rb_ref, bucket_ref, o_ref):
    for k in range(2):
        bk = bucket_ref[k]
        for h in range(A_HEADS):
            far = rb_ref[REL_BUCKETS - 1, h]
            acc = jnp.zeros(bk.shape, F32)
            for b in range(REL_BUCKETS - 1):
                acc = jnp.where(bk == b, rb_ref[b, h] - far, acc)
            o_ref[k, h] = acc


def _bias_tables(rel_bias):
    r = np.arange(Q_BLOCK)[:, None]
    c = np.arange(NEAR_W)[None, :]
    buckets = np.stack([_t5_bucket_np(r + Q_BLOCK - c), _t5_bucket_np(r - c)])
    return pl.pallas_call(
        _bias_kernel,
        out_shape=jax.ShapeDtypeStruct((2, A_HEADS, Q_BLOCK, NEAR_W), F32),
        in_specs=[pl.BlockSpec(memory_space=pltpu.SMEM), pl.BlockSpec(memory_space=pltpu.VMEM)],
        out_specs=pl.BlockSpec(memory_space=pltpu.VMEM),
        name="bias_tables",
    )(rel_bias, jnp.asarray(buckets))


def _key_to_f32(k):
    return lax.bitcast_convert_type(jnp.where(k < 0, INT_MIN - k, k), F32)


def _dsa_kernel(ql_ref, iqs_ref, iw_ref, ikt_ref, ckv_ref, tbl_ref, wuv_ref, o_ref,
                sc_ref, qs_ref, m_ref, l_ref, acc_ref, *, k_sel, seq):
    tq = Q_BLOCK
    H = A_HEADS
    i = pl.program_id(1)
    row0 = i * tq
    n_idx = (row0 + tq + IDX_TK - 1) // IDX_TK

    iw = iw_ref[...]
    w_cols = [iw[:, h:h + 1] for h in range(IDX_HEADS)]
    rows = row0 + lax.broadcasted_iota(jnp.int32, (tq, IDX_TK), 0)
    cols0 = lax.broadcasted_iota(jnp.int32, (tq, IDX_TK), 1)

    def idx_body(j, carry):
        c0 = pl.multiple_of(j * IDX_TK, IDX_TK)
        d = jnp.dot(iqs_ref[...], ikt_ref[:, pl.ds(c0, IDX_TK)], preferred_element_type=F32)
        s = jnp.zeros((tq, IDX_TK), F32)
        for h in range(IDX_HEADS):
            s = s + jnp.maximum(d[h * tq:(h + 1) * tq], 0.0) * w_cols[h]
        s = jnp.where(cols0 + c0 <= rows, s, -jnp.inf)
        sc_ref[:, pl.ds(c0, IDX_TK)] = s
        return carry

    lax.fori_loop(0, n_idx, idx_body, 0)

    def count(pred):
        def body(j, acc):
            c0 = pl.multiple_of(j * IDX_TK, IDX_TK)
            ind = jnp.where(pred(sc_ref[:, pl.ds(c0, IDX_TK)], c0), 1.0, 0.0)
            for q in range(IDX_TK // LANES):
                acc = acc + ind[:, q * LANES:(q + 1) * LANES]
            return acc
        acc = lax.fori_loop(0, n_idx, body, jnp.zeros((tq, LANES), F32))
        return jnp.sum(acc, axis=1, keepdims=True)

    kf = float(k_sel)

    def bis_cond(st):
        return st[2] > 0

    def bis_body(st):
        lo, hi, _ = st
        d = hi - lo
        mid = lo + lax.shift_right_logical(d, 1) + (d & 1)
        thr = _key_to_f32(mid)
        cnt = count(lambda s, c0: s >= thr)
        active = lo < hi
        ge = cnt >= kf
        lo_n = jnp.where(active & ge, mid, lo)
        hi_n = jnp.where(active, jnp.where(ge, jnp.where(cnt == kf, mid, hi), mid - 1), hi)
        n_act = jnp.sum((lo_n < hi_n).astype(jnp.int32))
        return lo_n, hi_n, n_act

    lo0 = jnp.full((tq, 1), KEY_NEG_INF, jnp.int32)
    hi0 = jnp.full((tq, 1), KEY_POS_INF, jnp.int32)
    lo, _, _ = lax.while_loop(bis_cond, bis_body, (lo0, hi0, jnp.int32(tq)))
    thr = _key_to_f32(lo)

    cnt_ge = count(lambda s, c0: s >= thr)
    n_tie = jnp.sum((cnt_ge > kf).astype(jnp.int32))

    @pl.when(n_tie > 0)
    def _():
        need = kf - count(lambda s, c0: s > thr)

        def pos_body(_, st):
            plo, phi = st
            mid = (plo + phi) // 2
            c = count(lambda s, c0: (s == thr) & (cols0 + c0 <= mid))
            ok = c >= need
            return jnp.where(ok, plo, mid + 1), jnp.where(ok, mid, phi)

        plo0 = jnp.zeros((tq, 1), jnp.int32)
        phi0 = jnp.full((tq, 1), seq - 1, jnp.int32)
        last, _ = lax.fori_loop(0, max(1, (seq - 1).bit_length()), pos_body, (plo0, phi0))

        def drop_body(j, carry):
            c0 = pl.multiple_of(j * IDX_TK, IDX_TK)
            s = sc_ref[:, pl.ds(c0, IDX_TK)]
            sc_ref[:, pl.ds(c0, IDX_TK)] = jnp.where((s == thr) & (cols0 + c0 > last), -jnp.inf, s)
            return carry

        lax.fori_loop(0, n_idx, drop_body, 0)

    for h in range(H):
        qs_ref[h * tq:(h + 1) * tq, :] = ql_ref[:, h * KV_RANK:(h + 1) * KV_RANK]
    m_ref[...] = jnp.full(m_ref.shape, M_INIT, F32)
    l_ref[...] = jnp.zeros(l_ref.shape, F32)
    acc_ref[...] = jnp.zeros(acc_ref.shape, F32)

    def attend(lg3, sel, ckv_t):
        w = lg3.shape[-1]
        lgm = jnp.where(sel[None], lg3, NEG)
        m_prev = m_ref[...]
        m_new = jnp.maximum(m_prev, jnp.max(lgm, axis=-1, keepdims=True))
        p = jnp.exp(lgm - m_new)
        alpha = jnp.exp(m_prev - m_new)
        l_ref[...] = alpha * l_ref[...] + jnp.sum(p, axis=-1, keepdims=True)
        m_ref[...] = m_new
        pv = jnp.dot(p.reshape(H * tq, w).astype(BF16), ckv_t, preferred_element_type=F32)
        acc_ref[...] = alpha.reshape(H * tq, 1) * acc_ref[...] + pv

    nt_dims = (((1,), (1,)), ((), ()))
    near0 = jnp.maximum(row0 - tq, 0)
    n_far = (near0 + ATT_TK - 1) // ATT_TK
    fcols0 = lax.broadcasted_iota(jnp.int32, (tq, ATT_TK), 1)

    def far_body(j, carry):
        c0 = pl.multiple_of(j * ATT_TK, ATT_TK)
        ckv_t = ckv_ref[pl.ds(c0, ATT_TK), :]
        lg = lax.dot_general(qs_ref[...], ckv_t, nt_dims, preferred_element_type=F32)
        sel = (sc_ref[:, pl.ds(c0, ATT_TK)] >= thr) & (fcols0 + c0 < near0)
        attend(lg.reshape(H, tq, ATT_TK), sel, ckv_t)
        return carry

    lax.fori_loop(0, n_far, far_body, 0)

    n0 = pl.multiple_of(near0, tq)
    ckv_n = ckv_ref[pl.ds(n0, NEAR_W), :]
    lg = lax.dot_general(qs_ref[...], ckv_n, nt_dims, preferred_element_type=F32)
    first = (i == 0).astype(jnp.int32)
    lg3 = lg.reshape(H, tq, NEAR_W) + tbl_ref[first]
    ncol = n0 + lax.broadcasted_iota(jnp.int32, (tq, NEAR_W), 1)
    nrow = row0 + lax.broadcasted_iota(jnp.int32, (tq, NEAR_W), 0)
    sel = (sc_ref[:, pl.ds(n0, NEAR_W)] >= thr) & (ncol <= nrow)
    attend(lg3, sel, ckv_n)

    o = acc_ref[...] / l_ref[...].reshape(H * tq, 1)
    o_cat = jnp.concatenate([o[h * tq:(h + 1) * tq] for h in range(H)], axis=1).astype(BF16)
    o_ref[...] = jnp.dot(o_cat, wuv_ref[...], preferred_element_type=F32).astype(BF16)


def _dsa(ql, iqs, iw, ikt, ckv, tbl, wuv, layer, B, S):
    tq = Q_BLOCK
    nb = S // tq
    RL = A_HEADS * KV_RANK
    k_sel = min(TOPK_MAX, S // 4)
    return pl.pallas_call(
        functools.partial(_dsa_kernel, k_sel=k_sel, seq=S),
        out_shape=jax.ShapeDtypeStruct((B * S, A_WIDTH), BF16),
        grid=(B, nb),
        in_specs=[
            pl.BlockSpec((tq, RL), lambda b, i: (b * nb + i, 0)),
            pl.BlockSpec((None, IDX_HEADS * tq, IDX_DIM), lambda b, i: (b * nb + i, 0, 0)),
            pl.BlockSpec((tq, IDX_HEADS), lambda b, i: (b * nb + i, 0)),
            pl.BlockSpec((None, IDX_DIM, S), lambda b, i: (b, 0, 0)),
            pl.BlockSpec((None, S, KV_RANK), lambda b, i: (b, 0, 0)),
            pl.BlockSpec((2, A_HEADS, tq, NEAR_W), lambda b, i: (0, 0, 0, 0)),
            pl.BlockSpec((None, RL, A_WIDTH), lambda b, i: (layer, 0, 0)),
        ],
        out_specs=pl.BlockSpec((tq, A_WIDTH), lambda b, i: (b * nb + i, 0)),
        scratch_shapes=[
            pltpu.VMEM((tq, S), F32),
            pltpu.VMEM((A_HEADS * tq, KV_RANK), BF16),
            pltpu.VMEM((A_HEADS, tq, 1), F32),
            pltpu.VMEM((A_HEADS, tq, 1), F32),
            pltpu.VMEM((A_HEADS * tq, KV_RANK), F32),
        ],
        compiler_params=_params("parallel", "arbitrary"),
        name="dsa",
    )(ql, iqs, iw, ikt, ckv, tbl, wuv)


def _memkv_kernel(mem_ref, g_ref, w_ref, kbd_ref, vbd_ref):
    M = mem_ref.shape[0]
    mn = _rms(mem_ref[...], g_ref[...]).astype(BF16)
    kv = jnp.dot(mn, w_ref[...], preferred_element_type=F32)
    kt = kv[:, :C_WIDTH].T
    v = kv[:, C_WIDTH:]
    r = lax.broadcasted_iota(jnp.int32, (C_WIDTH, C_HEADS * M), 0) // C_HEAD_DIM
    c = lax.broadcasted_iota(jnp.int32, (C_WIDTH, C_HEADS * M), 1) // M
    kbd_ref[...] = jnp.where(r == c, jnp.concatenate([kt] * C_HEADS, axis=1), 0.0).astype(BF16)
    r = lax.broadcasted_iota(jnp.int32, (C_HEADS * M, C_WIDTH), 0) // M
    c = lax.broadcasted_iota(jnp.int32, (C_HEADS * M, C_WIDTH), 1) // C_HEAD_DIM
    vbd_ref[...] = jnp.where(r == c, jnp.concatenate([v] * C_HEADS, axis=0), 0.0).astype(BF16)


def _memkv(mem, g, w, layer):
    B, M, D = mem.shape
    return pl.pallas_call(
        _memkv_kernel,
        out_shape=[jax.ShapeDtypeStruct((B, C_WIDTH, C_HEADS * M), BF16),
                   jax.ShapeDtypeStruct((B, C_HEADS * M, C_WIDTH), BF16)],
        grid=(B,),
        in_specs=[pl.BlockSpec((None, M, D), lambda b: (b, 0, 0)),
                  pl.BlockSpec((None, 1, D), lambda b: (layer, 0, 0)),
                  pl.BlockSpec((None, D, 2 * C_WIDTH), lambda b: (layer, 0, 0))],
        out_specs=[pl.BlockSpec((None, C_WIDTH, C_HEADS * M), lambda b: (b, 0, 0)),
                   pl.BlockSpec((None, C_HEADS * M, C_WIDTH), lambda b: (b, 0, 0))],
        compiler_params=_params("parallel"),
        name="memkv",
    )(mem, g, w)


def _mix_kernel(h_ref, g_ref, wg_ref, ya_ref, z_ref, zh_ref, qc_ref, kbd_ref, vbd_ref, wpool_ref, ps_ref,
                wa_ref, wb_ref, wc_ref, wo_ref, o_ref, zx_ref):
    tm, D = h_ref.shape
    i = pl.program_id(1)
    M = kbd_ref.shape[1] // C_HEADS
    h = h_ref[...]
    u = _rms(h, g_ref[...]).astype(BF16)

    z = z_ref[...]
    zx_ref[0:POOL_HALO, :] = jnp.where(i == 0, 0.0, zh_ref[...])
    zx_ref[POOL_HALO:, :] = z
    pos = i * tm + lax.broadcasted_iota(jnp.int32, (tm, 1), 0) + 1
    grp = lax.broadcasted_iota(jnp.int32, (1, POOL_WIDTH), 1) // POOL_GROUP
    win = z
    pooled = jnp.zeros_like(z)
    k = 1
    for gi, w in enumerate(POOL_WINDOWS):
        while k < w:
            win = win + zx_ref[POOL_HALO - k:POOL_HALO - k + tm, :]
            k += 1
        cnt = jnp.minimum(pos, w).astype(F32)
        pooled = jnp.where(grp == gi, win / cnt - z, pooled)
    yb = jnp.dot(pooled.astype(BF16), wpool_ref[...], preferred_element_type=F32) * ps_ref[...]

    lg = jnp.dot(qc_ref[...], kbd_ref[...], preferred_element_type=F32)
    ps = []
    for hc in range(C_HEADS):
        seg = lg[:, hc * M:(hc + 1) * M]
        e = jnp.exp(seg - jnp.max(seg, axis=-1, keepdims=True))
        ps.append(e / jnp.sum(e, axis=-1, keepdims=True))
    p = jnp.concatenate(ps, axis=1).astype(BF16)
    yc = jnp.dot(p, vbd_ref[...], preferred_element_type=F32)

    def gate(n):
        pre = jnp.dot(u, wg_ref[:, n * D:(n + 1) * D], preferred_element_type=F32)
        return jax.nn.sigmoid(pre)

    merged = gate(0) * jnp.dot(ya_ref[...], wa_ref[...], preferred_element_type=F32)
    merged += gate(1) * jnp.dot(yb.astype(BF16), wb_ref[...], preferred_element_type=F32)
    merged += gate(2) * jnp.dot(yc.astype(BF16), wc_ref[...], preferred_element_type=F32)
    o_ref[...] = h + jnp.dot(merged.astype(BF16), wo_ref[...], preferred_element_type=F32)


def _mix(h, g, ya, z, qc, kbd, vbd, wp, layer, B, S):
    T, D = h.shape
    tm = min(MIX_TM, S)
    nt = S // tm
    hb = tm // POOL_HALO

    def wspec(arr):
        return pl.BlockSpec((None,) + arr.shape[1:], lambda b, i: (layer,) + (0,) * (arr.ndim - 1))

    tok = lambda w: pl.BlockSpec((tm, w), lambda b, i: (b * nt + i, 0))
    names = ("wpool", "ps", "wa", "wb", "wc", "wo")
    return pl.pallas_call(
        _mix_kernel,
        out_shape=jax.ShapeDtypeStruct((T, D), F32),
        grid=(B, nt),
        in_specs=[tok(D), wspec(g), wspec(wp["wg"]), tok(A_WIDTH), tok(POOL_WIDTH),
                  pl.BlockSpec((POOL_HALO, POOL_WIDTH), lambda b, i: (jnp.maximum((b * nt + i) * hb - 1, 0), 0)),
                  tok(C_WIDTH),
                  pl.BlockSpec((None,) + kbd.shape[1:], lambda b, i: (b, 0, 0)),
                  pl.BlockSpec((None,) + vbd.shape[1:], lambda b, i: (b, 0, 0))]
                 + [wspec(wp[k]) for k in names],
        out_specs=tok(D),
        scratch_shapes=[pltpu.VMEM((tm + POOL_HALO, POOL_WIDTH), F32)],
        compiler_params=_params("parallel", "parallel"),
        name="mix",
    )(h, g, wp["wg"], ya, z, z, qc, kbd, vbd, *[wp[k] for k in names])


def _block_diag(blocks):
    L, G, r, c = blocks.shape
    eye = jnp.eye(G, dtype=blocks.dtype)
    return jnp.einsum("lgrc,gh->lgrhc", blocks, eye).reshape(L, G * r, G * c)


def _prep_weights(w_in, kv_norm, w_uk, w_uv, w_pool, pool_scale, w_branch_a, w_branch_b, w_branch_c, w_out):
    L, D, _ = w_in.shape
    sizes = (A_WIDTH, KV_RANK, IDX_HEADS * IDX_DIM, IDX_DIM, IDX_HEADS, POOL_WIDTH, C_WIDTH, N_BRANCH * D)
    offs = np.concatenate([[0], np.cumsum(sizes)])
    wq, wckv, wiq, wik, wiw, wz, wqc, wg = [w_in[:, :, offs[n]:offs[n + 1]] for n in range(len(sizes))]
    wiw_pad = jnp.pad(wiw, ((0, 0), (0, 0), (0, LANES - IDX_HEADS)))
    proj = dict(
        wq=wq.astype(BF16),
        wuk=_block_diag(jnp.transpose(w_uk, (0, 2, 3, 1))).astype(BF16),
        wckv=wckv.astype(BF16),
        kvg=kv_norm[:, None, :],
        wiq=wiq.astype(BF16),
        wikt=jnp.transpose(wik, (0, 2, 1)).astype(BF16),
        wiw=wiw_pad.astype(BF16),
        wz=wz.astype(BF16),
        wqc=wqc.astype(BF16),
    )
    wuv = _block_diag(jnp.transpose(w_uv, (0, 2, 1, 3))).astype(BF16)
    mix = dict(
        wg=wg.astype(BF16),
        wpool=_block_diag(w_pool).astype(BF16),
        ps=pool_scale[:, None, :],
        wa=w_branch_a.astype(BF16), wb=w_branch_b.astype(BF16), wc=w_branch_c.astype(BF16),
        wo=w_out.astype(BF16),
    )
    return proj, wuv, mix


def kernel(x, mem, rel_bias, ffn1_norm, ffn1_w_in, ffn1_w_out, mix_norm, w_in, kv_norm, w_uk, w_uv, w_pool,
           pool_scale, mem_norm, w_mem_kv, w_branch_a, w_branch_b, w_branch_c, w_out, ffn2_norm, ffn2_w_in,
           ffn2_w_out, final_norm):
    B, S, D = x.shape
    L = w_in.shape[0]
    assert S % Q_BLOCK == 0 and S % IDX_TK == 0 and S >= NEAR_W
    nb = S // Q_BLOCK

    proj_w, wuv, mix_w = _prep_weights(w_in, kv_norm, w_uk, w_uv, w_pool, pool_scale,
                                       w_branch_a, w_branch_b, w_branch_c, w_out)
    f1_in, f1_out = ffn1_w_in.astype(BF16), ffn1_w_out.astype(BF16)
    f2_in, f2_out = ffn2_w_in.astype(BF16), ffn2_w_out.astype(BF16)
    f1_g, f2_g, mx_g, mm_g = (a[:, None, :] for a in (ffn1_norm, ffn2_norm, mix_norm, mem_norm))
    fg = final_norm[None, :]
    wmem = w_mem_kv.astype(BF16)
    tbl = _bias_tables(rel_bias)

    h = x.reshape(B * S, D)
    for l in range(L):
        h = _ffn(h, f1_g, f1_in, f1_out, fg, l, final=False)
        ql, ckv, iq, ikt, iw, z, qc = _proj(h, mx_g, proj_w, l, B, S)
        iqs = iq.reshape(B * nb, Q_BLOCK, IDX_HEADS, IDX_DIM).transpose(0, 2, 1, 3)
        iqs = iqs.reshape(B * nb, IDX_HEADS * Q_BLOCK, IDX_DIM)
        ya = _dsa(ql, iqs, iw, ikt, ckv.reshape(B, S, KV_RANK), tbl, wuv, l, B, S)
        kbd, vbd = _memkv(mem, mm_g, wmem, l)
        h = _mix(h, mx_g, ya, z, qc, kbd, vbd, mix_w, l, B, S)
        h = _ffn(h, f2_g, f2_in, f2_out, fg, l, final=(l == L - 1))
    return h.reshape(B, S, D)
```

```python
import functools
import math

import numpy as np
import jax
import jax.numpy as jnp
from jax import lax
from jax.experimental import pallas as pl
from jax.experimental.pallas import tpu as pltpu

F32 = jnp.float32
BF16 = jnp.bfloat16

A_HEADS = 8
A_HEAD_DIM = 64
A_WIDTH = A_HEADS * A_HEAD_DIM
KV_RANK = 128
IDX_HEADS = 8
IDX_DIM = 32
TOPK_MAX = 256
Q_BLOCK = 128
POOL_WINDOWS = (2, 4, 8, 16)
POOL_GROUP = 64
POOL_WIDTH = len(POOL_WINDOWS) * POOL_GROUP
POOL_HALO = max(POOL_WINDOWS)
C_HEADS = 4
C_HEAD_DIM = 64
C_WIDTH = C_HEADS * C_HEAD_DIM
N_BRANCH = 3
REL_BUCKETS = 32
REL_MAX_DIST = 128
EPS = 1e-6
NEG = -1e30
M_INIT = -1e29
LOG2E = math.log2(math.e)

LANES = 128
SUBLANES = 8
BF16_ROWS = 16
VMEM_LIMIT = 52 * 1024 * 1024
FFN_TM = 512
PROJ_TM = 512
MIX_TM = 512
IDX_TK = 512
ATT_TK = 512
NEAR_W = 2 * Q_BLOCK
BISECT_FIXED_STEPS = 8

INT_MIN = -2 ** 31


def _rms(x, g):
    return x * lax.rsqrt(jnp.mean(x * x, axis=-1, keepdims=True) + EPS) * g


def _params(*sem):
    return pltpu.CompilerParams(dimension_semantics=sem, vmem_limit_bytes=VMEM_LIMIT)


def _dot_nt(a, b):
    return lax.dot_general(a, b, (((1,), (1,)), ((), ())), preferred_element_type=F32)


def _ffn_kernel(x_ref, g_ref, wa_ref, wb_ref, wo_ref, fg_ref, o_ref, xn_ref, acc_ref, *, n_ff, final):
    j = pl.program_id(1)

    @pl.when(j == 0)
    def _():
        xn_ref[...] = _rms(x_ref[...], g_ref[...]).astype(BF16)
        acc_ref[...] = jnp.zeros_like(acc_ref)

    xn = xn_ref[...]
    a = jnp.dot(xn, wa_ref[...], preferred_element_type=F32)
    b = jnp.dot(xn, wb_ref[...], preferred_element_type=F32)
    act = (a * jax.nn.sigmoid(a) * b).astype(BF16)
    acc_ref[...] += jnp.dot(act, wo_ref[...], preferred_element_type=F32)

    @pl.when(j == n_ff - 1)
    def _():
        y = x_ref[...] + 0.5 * acc_ref[...]
        if final:
            y = _rms(y, fg_ref[...])
        o_ref[...] = y


def _ffn(h, g, w_in, w_out, fg, layer, *, final):
    T, D = h.shape
    F = w_out.shape[1]
    n_ff = 2 if F % (2 * LANES) == 0 else 1
    fc = F // n_ff
    tm = min(FFN_TM, T)
    return pl.pallas_call(
        functools.partial(_ffn_kernel, n_ff=n_ff, final=final),
        out_shape=jax.ShapeDtypeStruct((T, D), F32),
        grid=(T // tm, n_ff),
        in_specs=[
            pl.BlockSpec((tm, D), lambda i, j: (i, 0)),
            pl.BlockSpec((None, 1, D), lambda i, j: (layer, 0, 0)),
            pl.BlockSpec((None, D, fc), lambda i, j: (layer, 0, j)),
            pl.BlockSpec((None, D, fc), lambda i, j: (layer, 0, n_ff + j)),
            pl.BlockSpec((None, fc, D), lambda i, j: (layer, j, 0)),
            pl.BlockSpec((1, D), lambda i, j: (0, 0)),
        ],
        out_specs=pl.BlockSpec((tm, D), lambda i, j: (i, 0)),
        scratch_shapes=[pltpu.VMEM((tm, D), BF16), pltpu.VMEM((tm, D), F32)],
        compiler_params=_params("parallel", "arbitrary"),
        name="ffn",
    )(h, g, w_in, w_in, w_out, fg)


_PROJ_W = ("wq", "wukt", "wckv", "kvg", "wiqt", "wik", "wiwt", "wz", "wqc")


def _proj_kernel(h_ref, g_ref, wq_ref, wukt_ref, wckv_ref, kvg_ref, wiqt_ref, wik_ref, wiwt_ref, wz_ref, wqc_ref,
                 qlt_ref, ckv_ref, ckvt_ref, iqt_ref, ik_ref, iwt_ref, z_ref, qc_ref):
    u = _rms(h_ref[...], g_ref[...]).astype(BF16)
    q = jnp.dot(u, wq_ref[...], preferred_element_type=F32).astype(BF16)
    qlt_ref[...] = (_dot_nt(wukt_ref[...], q) * (A_HEAD_DIM ** -0.5 * LOG2E)).astype(BF16)
    c = _rms(jnp.dot(u, wckv_ref[...], preferred_element_type=F32), kvg_ref[...])
    ckv_ref[...] = c.astype(BF16)
    ckvt_ref[...] = c.T.astype(BF16)
    iqt_ref[...] = _dot_nt(wiqt_ref[...], u).astype(BF16)
    ik_ref[...] = jnp.dot(u, wik_ref[...], preferred_element_type=F32)[:, :IDX_DIM].astype(BF16)
    iwt = _dot_nt(wiwt_ref[...], u)[:IDX_HEADS]
    iwt_ref[...] = iwt * ((IDX_DIM ** -0.5) * (IDX_HEADS ** -0.5))
    z_ref[...] = jnp.dot(u, wz_ref[...], preferred_element_type=F32)
    qc = jnp.dot(u, wqc_ref[...], preferred_element_type=F32) * (C_HEAD_DIM ** -0.5)
    qc_ref[...] = qc.astype(BF16)


def _proj(h, g, wp, layer, B, S):
    T, D = h.shape
    tm = min(PROJ_TM, S)
    nt = S // tm
    RL = A_HEADS * KV_RANK
    QI = IDX_HEADS * IDX_DIM

    def wspec(arr):
        return pl.BlockSpec((None,) + arr.shape[1:], lambda b, i: (layer,) + (0,) * (arr.ndim - 1))

    tok = lambda w: pl.BlockSpec((tm, w), lambda b, i: (b * nt + i, 0))
    tokt = lambda w: pl.BlockSpec((None, w, tm), lambda b, i: (b, 0, i))
    return pl.pallas_call(
        _proj_kernel,
        out_shape=[
            jax.ShapeDtypeStruct((B, RL, S), BF16),
            jax.ShapeDtypeStruct((T, KV_RANK), BF16),
            jax.ShapeDtypeStruct((B, KV_RANK, S), BF16),
            jax.ShapeDtypeStruct((B, QI, S), BF16),
            jax.ShapeDtypeStruct((T, IDX_DIM), BF16),
            jax.ShapeDtypeStruct((B, IDX_HEADS, S), F32),
            jax.ShapeDtypeStruct((T, POOL_WIDTH), F32),
            jax.ShapeDtypeStruct((T, C_WIDTH), BF16),
        ],
        grid=(B, nt),
        in_specs=[tok(D), wspec(g)] + [wspec(wp[k]) for k in _PROJ_W],
        out_specs=[tokt(RL), tok(KV_RANK), tokt(KV_RANK), tokt(QI), tok(IDX_DIM), tokt(IDX_HEADS),
                   tok(POOL_WIDTH), tok(C_WIDTH)],
        compiler_params=_params("parallel", "parallel"),
        name="proj",
    )(h, g, *[wp[k] for k in _PROJ_W])


def _t5_bucket_np(dist):
    max_exact = REL_BUCKETS // 2
    n = np.maximum(dist, 0)
    nf = np.maximum(n, 1).astype(np.float32)
    large = max_exact + (np.log(nf / np.float32(max_exact)) / np.float32(math.log(REL_MAX_DIST / max_exact))
                         * np.float32(REL_BUCKETS - max_exact)).astype(np.int32)
    large = np.minimum(large, REL_BUCKETS - 1)
    return np.where(n < max_exact, n, large).astype(np.int32)


def _bias_kernel(rb_ref, bucket_ref, o_ref):
    for k in range(2):
        bk = bucket_ref[k]
        for h in range(A_HEADS):
            far = rb_ref[REL_BUCKETS - 1, h]
            acc = jnp.zeros(bk.shape, F32)
            for b in range(REL_BUCKETS - 1):
                acc = jnp.where(bk == b, (rb_ref[b, h] - far) * LOG2E, acc)
            o_ref[k, :, h * Q_BLOCK:(h + 1) * Q_BLOCK] = acc


def _bias_tables(rel_bias):
    c = np.arange(NEAR_W)[:, None]
    r = np.arange(Q_BLOCK)[None, :]
    buckets = np.stack([_t5_bucket_np(r + Q_BLOCK - c), _t5_bucket_np(r - c)])
    return pl.pallas_call(
        _bias_kernel,
        out_shape=jax.ShapeDtypeStruct((2, NEAR_W, A_HEADS * Q_BLOCK), F32),
        in_specs=[pl.BlockSpec(memory_space=pltpu.SMEM), pl.BlockSpec(memory_space=pltpu.VMEM)],
        out_specs=pl.BlockSpec(memory_space=pltpu.VMEM),
        name="bias_tables",
    )(rel_bias, jnp.asarray(buckets))


def _key_to_f32(k):
    return lax.bitcast_convert_type(jnp.where(k < 0, INT_MIN - k, k), F32)


def _f32_to_key(x):
    b = lax.bitcast_convert_type(x, jnp.int32)
    return jnp.where(b < 0, INT_MIN - b, b)


def _dsa_kernel(qlt_ref, iqt_ref, iwt_ref, ik_ref, ckv_ref, ckvt_ref, tbl_ref, wuvt_ref, o_ref,
                sc_ref, qs_ref, iqs_ref, m_ref, l_ref, acc_ref, fold_ref, *, k_sel, seq):
    tq = Q_BLOCK
    H = A_HEADS
    i = pl.program_id(1)
    row0 = i * tq
    n_idx = (row0 + tq + IDX_TK - 1) // IDX_TK

    eye = (lax.broadcasted_iota(jnp.int32, (tq, tq), 0) == lax.broadcasted_iota(jnp.int32, (tq, tq), 1))
    eye = jnp.where(eye, 1.0, 0.0).astype(BF16)
    for h in range(H):
        qs_ref[0:KV_RANK, h * tq:(h + 1) * tq] = qlt_ref[h * KV_RANK:(h + 1) * KV_RANK, :]
        qs_ref[KV_RANK:, h * tq:(h + 1) * tq] = eye
        iqs_ref[:, h * tq:(h + 1) * tq] = iqt_ref[h * IDX_DIM:(h + 1) * IDX_DIM, :]

    w_rows = [iwt_ref[h:h + 1, :] for h in range(IDX_HEADS)]
    qid = row0 + lax.broadcasted_iota(jnp.int32, (IDX_TK, tq), 1)
    kid0 = lax.broadcasted_iota(jnp.int32, (IDX_TK, tq), 0)
    fold_ref[...] = jnp.full(fold_ref.shape, -jnp.inf, F32)

    def idx_body(j, carry):
        r0 = pl.multiple_of(j * IDX_TK, IDX_TK)
        d = jnp.dot(ik_ref[pl.ds(r0, IDX_TK), :], iqs_ref[...], preferred_element_type=F32)
        s = jnp.maximum(d[:, 0:tq], 0.0) * w_rows[0]
        for h in range(1, IDX_HEADS):
            s = s + jnp.maximum(d[:, h * tq:(h + 1) * tq], 0.0) * w_rows[h]
        s = jnp.where(kid0 + r0 <= qid, s, -jnp.inf)
        sc_ref[pl.ds(r0, IDX_TK), :] = s
        f = s[0:k_sel]
        for q in range(1, IDX_TK // k_sel):
            f = jnp.maximum(f, s[q * k_sel:(q + 1) * k_sel])
        fold_ref[...] = jnp.maximum(fold_ref[...], f)
        return carry

    lax.fori_loop(0, n_idx, idx_body, 0)

    n_par = 8

    def count(pred):
        def body(j, acc):
            r0 = pl.multiple_of(j * IDX_TK, IDX_TK)
            ind = jnp.where(pred(sc_ref[pl.ds(r0, IDX_TK), :], r0), 1.0, 0.0)
            return acc + jnp.sum(ind.reshape(n_par, IDX_TK // n_par, tq), axis=0)
        acc = lax.fori_loop(0, n_idx, body, jnp.zeros((IDX_TK // n_par, tq), F32))
        return jnp.sum(acc, axis=0, keepdims=True)

    kf = float(k_sel)
    fold = fold_ref[...]
    lo0 = _f32_to_key(jnp.min(fold, axis=0, keepdims=True))
    hi0 = _f32_to_key(jnp.max(fold, axis=0, keepdims=True))

    def n_open(lo, hi):
        return jnp.sum((lo < hi).astype(F32))

    def bis_step(lo, hi):
        d = hi - lo
        mid = lo + lax.shift_right_logical(d, 1) + (d & 1)
        thr = _key_to_f32(mid)
        cnt = count(lambda s, r0: s >= thr)
        active = lo < hi
        ge = cnt >= kf
        lo_n = jnp.where(active & ge, mid, lo)
        hi_n = jnp.where(active, jnp.where(ge, jnp.where(cnt == kf, mid, hi), mid - 1), hi)
        return lo_n, hi_n

    def bis_body(st):
        lo_n, hi_n = bis_step(st[0], st[1])
        return lo_n, hi_n, n_open(lo_n, hi_n)

    lo, hi = lax.fori_loop(0, BISECT_FIXED_STEPS, lambda _, st: bis_step(*st), (lo0, hi0))
    lo, _, _ = lax.while_loop(lambda st: st[2] > 0.0, bis_body, (lo, hi, n_open(lo, hi)))
    thr = _key_to_f32(lo)

    cnt_ge = count(lambda s, r0: s >= thr)
    n_tie = jnp.sum((cnt_ge > kf).astype(F32))

    @pl.when(n_tie > 0.0)
    def _():
        need = kf - count(lambda s, r0: s > thr)

        def pos_body(_, st):
            plo, phi = st
            mid = (plo + phi) // 2
            c = count(lambda s, r0: (s == thr) & (kid0 + r0 <= mid))
            ok = c >= need
            return jnp.where(ok, plo, mid + 1), jnp.where(ok, mid, phi)

        plo0 = jnp.zeros((1, tq), jnp.int32)
        phi0 = jnp.full((1, tq), seq - 1, jnp.int32)
        last, _ = lax.fori_loop(0, max(1, (seq - 1).bit_length()), pos_body, (plo0, phi0))

        def drop_body(j, carry):
            r0 = pl.multiple_of(j * IDX_TK, IDX_TK)
            s = sc_ref[pl.ds(r0, IDX_TK), :]
            sc_ref[pl.ds(r0, IDX_TK), :] = jnp.where((s == thr) & (kid0 + r0 > last), -jnp.inf, s)
            return carry

        lax.fori_loop(0, n_idx, drop_body, 0)

    m_ref[...] = jnp.full(m_ref.shape, M_INIT, F32)
    l_ref[...] = jnp.zeros(l_ref.shape, F32)
    acc_ref[...] = jnp.zeros(acc_ref.shape, F32)

    def attend(ckv_t, ckvt_t, keep, bias):
        lhs = jnp.concatenate([ckv_t, jnp.where(keep, 0.0, NEG).astype(BF16)], axis=1)
        lg = jnp.dot(lhs, qs_ref[...], preferred_element_type=F32)
        if bias is not None:
            lg = lg + bias
        m_prev = m_ref[...]
        m_new = jnp.maximum(m_prev, jnp.max(lg, axis=0, keepdims=True))
        alpha = jnp.exp2(m_prev - m_new)
        p = jnp.exp2(lg - m_new)
        l_ref[...] = alpha * l_ref[...] + jnp.sum(p, axis=0, keepdims=True)
        m_ref[...] = m_new
        pv = jnp.dot(ckvt_t, p.astype(BF16), preferred_element_type=F32)
        acc_ref[...] = alpha * acc_ref[...] + pv

    near0 = jnp.maximum(row0 - tq, 0)
    n_far = (near0 + ATT_TK - 1) // ATT_TK
    fkid0 = lax.broadcasted_iota(jnp.int32, (ATT_TK, tq), 0)

    def far_body(j, carry):
        r0 = pl.multiple_of(j * ATT_TK, ATT_TK)
        keep = (sc_ref[pl.ds(r0, ATT_TK), :] >= thr) & (fkid0 + r0 < near0)
        attend(ckv_ref[pl.ds(r0, ATT_TK), :], ckvt_ref[:, pl.ds(r0, ATT_TK)], keep, None)
        return carry

    lax.fori_loop(0, n_far, far_body, 0)

    n0 = pl.multiple_of(near0, tq)
    first = (i == 0).astype(jnp.int32)
    nkid = n0 + lax.broadcasted_iota(jnp.int32, (NEAR_W, tq), 0)
    nqid = row0 + lax.broadcasted_iota(jnp.int32, (NEAR_W, tq), 1)
    keep = (sc_ref[pl.ds(n0, NEAR_W), :] >= thr) & (nkid <= nqid)
    attend(ckv_ref[pl.ds(n0, NEAR_W), :], ckvt_ref[:, pl.ds(n0, NEAR_W)], keep, tbl_ref[first])

    ot = (acc_ref[...] * (1.0 / l_ref[...])).astype(BF16)
    o_st = jnp.concatenate([ot[:, h * tq:(h + 1) * tq] for h in range(H)], axis=0)
    yt = jnp.dot(wuvt_ref[...], o_st, preferred_element_type=F32)
    o_ref[...] = yt.T.astype(BF16)


def _dsa(qlt, iqt, iwt, ik, ckv, ckvt, tbl, wuvt, layer, B, S):
    tq = Q_BLOCK
    nb = S // tq
    RL = A_HEADS * KV_RANK
    QI = IDX_HEADS * IDX_DIM
    k_sel = min(TOPK_MAX, S // 4)
    assert IDX_TK % k_sel == 0 and k_sel % SUBLANES == 0
    return pl.pallas_call(
        functools.partial(_dsa_kernel, k_sel=k_sel, seq=S),
        out_shape=jax.ShapeDtypeStruct((B * S, A_WIDTH), BF16),
        grid=(B, nb),
        in_specs=[
            pl.BlockSpec((None, RL, tq), lambda b, i: (b, 0, i)),
            pl.BlockSpec((None, QI, tq), lambda b, i: (b, 0, i)),
            pl.BlockSpec((None, IDX_HEADS, tq), lambda b, i: (b, 0, i)),
            pl.BlockSpec((None, S, IDX_DIM), lambda b, i: (b, 0, 0)),
            pl.BlockSpec((None, S, KV_RANK), lambda b, i: (b, 0, 0)),
            pl.BlockSpec((None, KV_RANK, S), lambda b, i: (b, 0, 0)),
            pl.BlockSpec((2, NEAR_W, A_HEADS * tq), lambda b, i: (0, 0, 0)),
            pl.BlockSpec((None, A_WIDTH, RL), lambda b, i: (layer, 0, 0)),
        ],
        out_specs=pl.BlockSpec((tq, A_WIDTH), lambda b, i: (b * nb + i, 0)),
        scratch_shapes=[
            pltpu.VMEM((S, tq), F32),
            pltpu.VMEM((KV_RANK + tq, A_HEADS * tq), BF16),
            pltpu.VMEM((IDX_DIM, IDX_HEADS * tq), BF16),
            pltpu.VMEM((1, A_HEADS * tq), F32),
            pltpu.VMEM((1, A_HEADS * tq), F32),
            pltpu.VMEM((KV_RANK, A_HEADS * tq), F32),
            pltpu.VMEM((k_sel, tq), F32),
        ],
        compiler_params=_params("parallel", "arbitrary"),
        name="dsa",
    )(qlt, iqt, iwt, ik, ckv, ckvt, tbl, wuvt)


def _memkv_kernel(mem_ref, g_ref, w_ref, kbd_ref, vbd_ref):
    M = mem_ref.shape[0]
    mn = _rms(mem_ref[...], g_ref[...]).astype(BF16)
    kv = jnp.dot(mn, w_ref[...], preferred_element_type=F32)
    kt = kv[:, :C_WIDTH].T
    v = kv[:, C_WIDTH:]
    r = lax.broadcasted_iota(jnp.int32, (C_WIDTH, C_HEADS * M), 0) // C_HEAD_DIM
    c = lax.broadcasted_iota(jnp.int32, (C_WIDTH, C_HEADS * M), 1) // M
    kbd_ref[...] = jnp.where(r == c, jnp.concatenate([kt] * C_HEADS, axis=1), 0.0).astype(BF16)
    r = lax.broadcasted_iota(jnp.int32, (C_HEADS * M, C_WIDTH), 0) // M
    c = lax.broadcasted_iota(jnp.int32, (C_HEADS * M, C_WIDTH), 1) // C_HEAD_DIM
    vbd_ref[...] = jnp.where(r == c, jnp.concatenate([v] * C_HEADS, axis=0), 0.0).astype(BF16)


def _memkv(mem, g, w, layer):
    B, M, D = mem.shape
    return pl.pallas_call(
        _memkv_kernel,
        out_shape=[jax.ShapeDtypeStruct((B, C_WIDTH, C_HEADS * M), BF16),
                   jax.ShapeDtypeStruct((B, C_HEADS * M, C_WIDTH), BF16)],
        grid=(B,),
        in_specs=[pl.BlockSpec((None, M, D), lambda b: (b, 0, 0)),
                  pl.BlockSpec((None, 1, D), lambda b: (layer, 0, 0)),
                  pl.BlockSpec((None, D, 2 * C_WIDTH), lambda b: (layer, 0, 0))],
        out_specs=[pl.BlockSpec((None, C_WIDTH, C_HEADS * M), lambda b: (b, 0, 0)),
                   pl.BlockSpec((None, C_HEADS * M, C_WIDTH), lambda b: (b, 0, 0))],
        compiler_params=_params("parallel"),
        name="memkv",
    )(mem, g, w)


def _mix_kernel(h_ref, g_ref, wg_ref, ya_ref, z_ref, zh_ref, qc_ref, kbd_ref, vbd_ref, wpool_ref, ps_ref,
                wa_ref, wb_ref, wc_ref, wo_ref, o_ref, zx_ref):
    tm, D = h_ref.shape
    i = pl.program_id(1)
    M = kbd_ref.shape[1] // C_HEADS
    h = h_ref[...]
    u = _rms(h, g_ref[...]).astype(BF16)

    z = z_ref[...]
    zx_ref[0:POOL_HALO, :] = jnp.where(i == 0, 0.0, zh_ref[...])
    zx_ref[POOL_HALO:, :] = z
    pos = i * tm + lax.broadcasted_iota(jnp.int32, (tm, 1), 0) + 1
    grp = lax.broadcasted_iota(jnp.int32, (1, POOL_WIDTH), 1) // POOL_GROUP
    win = z
    pooled = jnp.zeros_like(z)
    k = 1
    for gi, w in enumerate(POOL_WINDOWS):
        while k < w:
            win = win + zx_ref[POOL_HALO - k:POOL_HALO - k + tm, :]
            k += 1
        cnt = jnp.minimum(pos, w).astype(F32)
        pooled = jnp.where(grp == gi, win / cnt - z, pooled)
    yb = jnp.dot(pooled.astype(BF16), wpool_ref[...], preferred_element_type=F32) * ps_ref[...]

    lg = jnp.dot(qc_ref[...], kbd_ref[...], preferred_element_type=F32)
    ps = []
    for hc in range(C_HEADS):
        seg = lg[:, hc * M:(hc + 1) * M]
        e = jnp.exp(seg - jnp.max(seg, axis=-1, keepdims=True))
        ps.append(e / jnp.sum(e, axis=-1, keepdims=True))
    p = jnp.concatenate(ps, axis=1).astype(BF16)
    yc = jnp.dot(p, vbd_ref[...], preferred_element_type=F32)

    def gate(n):
        pre = jnp.dot(u, wg_ref[:, n * D:(n + 1) * D], preferred_element_type=F32)
        return jax.nn.sigmoid(pre)

    merged = gate(0) * jnp.dot(ya_ref[...], wa_ref[...], preferred_element_type=F32)
    merged += gate(1) * jnp.dot(yb.astype(BF16), wb_ref[...], preferred_element_type=F32)
    merged += gate(2) * jnp.dot(yc.astype(BF16), wc_ref[...], preferred_element_type=F32)
    o_ref[...] = h + jnp.dot(merged.astype(BF16), wo_ref[...], preferred_element_type=F32)


def _mix(h, g, ya, z, qc, kbd, vbd, wp, layer, B, S):
    T, D = h.shape
    tm = min(MIX_TM, S)
    nt = S // tm
    hb = tm // POOL_HALO

    def wspec(arr):
        return pl.BlockSpec((None,) + arr.shape[1:], lambda b, i: (layer,) + (0,) * (arr.ndim - 1))

    tok = lambda w: pl.BlockSpec((tm, w), lambda b, i: (b * nt + i, 0))
    names = ("wpool", "ps", "wa", "wb", "wc", "wo")
    return pl.pallas_call(
        _mix_kernel,
        out_shape=jax.ShapeDtypeStruct((T, D), F32),
        grid=(B, nt),
        in_specs=[tok(D), wspec(g), wspec(wp["wg"]), tok(A_WIDTH), tok(POOL_WIDTH),
                  pl.BlockSpec((POOL_HALO, POOL_WIDTH), lambda b, i: (jnp.maximum((b * nt + i) * hb - 1, 0), 0)),
                  tok(C_WIDTH),
                  pl.BlockSpec((None,) + kbd.shape[1:], lambda b, i: (b, 0, 0)),
                  pl.BlockSpec((None,) + vbd.shape[1:], lambda b, i: (b, 0, 0))]
                 + [wspec(wp[k]) for k in names],
        out_specs=tok(D),
        scratch_shapes=[pltpu.VMEM((tm + POOL_HALO, POOL_WIDTH), F32)],
        compiler_params=_params("parallel", "parallel"),
        name="mix",
    )(h, g, wp["wg"], ya, z, z, qc, kbd, vbd, *[wp[k] for k in names])


def _block_diag(blocks):
    L, G, r, c = blocks.shape
    eye = jnp.eye(G, dtype=blocks.dtype)
    return jnp.einsum("lgrc,gh->lgrhc", blocks, eye).reshape(L, G * r, G * c)


def _prep_weights(w_in, kv_norm, w_uk, w_uv, w_pool, pool_scale, w_branch_a, w_branch_b, w_branch_c, w_out):
    L, D, _ = w_in.shape
    sizes = (A_WIDTH, KV_RANK, IDX_HEADS * IDX_DIM, IDX_DIM, IDX_HEADS, POOL_WIDTH, C_WIDTH, N_BRANCH * D)
    offs = np.concatenate([[0], np.cumsum(sizes)])
    wq, wckv, wiq, wik, wiw, wz, wqc, wg = [w_in[:, :, offs[n]:offs[n + 1]] for n in range(len(sizes))]
    proj = dict(
        wq=wq.astype(BF16),
        wukt=_block_diag(jnp.transpose(w_uk, (0, 2, 1, 3))).astype(BF16),
        wckv=wckv.astype(BF16),
        kvg=kv_norm[:, None, :],
        wiqt=jnp.transpose(wiq, (0, 2, 1)).astype(BF16),
        wik=jnp.pad(wik, ((0, 0), (0, 0), (0, LANES - IDX_DIM))).astype(BF16),
        wiwt=jnp.pad(jnp.transpose(wiw, (0, 2, 1)), ((0, 0), (0, BF16_ROWS - IDX_HEADS), (0, 0))).astype(BF16),
        wz=wz.astype(BF16),
        wqc=wqc.astype(BF16),
    )
    wuvt = _block_diag(jnp.transpose(w_uv, (0, 2, 3, 1))).astype(BF16)
    mix = dict(
        wg=wg.astype(BF16),
        wpool=_block_diag(w_pool).astype(BF16),
        ps=pool_scale[:, None, :],
        wa=w_branch_a.astype(BF16), wb=w_branch_b.astype(BF16), wc=w_branch_c.astype(BF16),
        wo=w_out.astype(BF16),
    )
    return proj, wuvt, mix


def kernel(x, mem, rel_bias, ffn1_norm, ffn1_w_in, ffn1_w_out, mix_norm, w_in, kv_norm, w_uk, w_uv, w_pool,
           pool_scale, mem_norm, w_mem_kv, w_branch_a, w_branch_b, w_branch_c, w_out, ffn2_norm, ffn2_w_in,
           ffn2_w_out, final_norm):
    B, S, D = x.shape
    L = w_in.shape[0]
    assert S % Q_BLOCK == 0 and S % IDX_TK == 0 and S % ATT_TK == 0 and S >= NEAR_W

    proj_w, wuvt, mix_w = _prep_weights(w_in, kv_norm, w_uk, w_uv, w_pool, pool_scale,
                                        w_branch_a, w_branch_b, w_branch_c, w_out)
    f1_in, f1_out = ffn1_w_in.astype(BF16), ffn1_w_out.astype(BF16)
    f2_in, f2_out = ffn2_w_in.astype(BF16), ffn2_w_out.astype(BF16)
    f1_g, f2_g, mx_g, mm_g = (a[:, None, :] for a in (ffn1_norm, ffn2_norm, mix_norm, mem_norm))
    fg = final_norm[None, :]
    wmem = w_mem_kv.astype(BF16)
    tbl = _bias_tables(rel_bias)

    h = x.reshape(B * S, D)
    for l in range(L):
        h = _ffn(h, f1_g, f1_in, f1_out, fg, l, final=False)
        qlt, ckv, ckvt, iqt, ik, iwt, z, qc = _proj(h, mx_g, proj_w, l, B, S)
        ya = _dsa(qlt, iqt, iwt, ik.reshape(B, S, IDX_DIM), ckv.reshape(B, S, KV_RANK), ckvt, tbl, wuvt, l, B, S)
        kbd, vbd = _memkv(mem, mm_g, wmem, l)
        h = _mix(h, mx_g, ya, z, qc, kbd, vbd, mix_w, l, B, S)
        h = _ffn(h, f2_g, f2_in, f2_out, fg, l, final=(l == L - 1))
    return h.reshape(B, S, D)
```

```python
import functools
import math

import numpy as np
import jax
import jax.numpy as jnp
from jax import lax
from jax.experimental import pallas as pl
from jax.experimental.pallas import tpu as pltpu

F32 = jnp.float32
BF16 = jnp.bfloat16

A_HEADS = 8
A_HEAD_DIM = 64
A_WIDTH = A_HEADS * A_HEAD_DIM
KV_RANK = 128
IDX_HEADS = 8
IDX_DIM = 32
TOPK_MAX = 256
Q_BLOCK = 128
POOL_WINDOWS = (2, 4, 8, 16)
POOL_GROUP = 64
POOL_WIDTH = len(POOL_WINDOWS) * POOL_GROUP
POOL_HALO = max(POOL_WINDOWS)
C_HEADS = 4
C_HEAD_DIM = 64
C_WIDTH = C_HEADS * C_HEAD_DIM
N_BRANCH = 3
REL_BUCKETS = 32
REL_MAX_DIST = 128
EPS = 1e-6
NEG = -1e30
M_INIT = -1e29
LOG2E = math.log2(math.e)
MASK_BIG = -NEG
SHIFT_MAX = 2.0 ** 20
L_TOT_MIN = 2.0 ** -60
L_TOT_MAX = 2.0 ** 100

LANES = 128
SUBLANES = 8
BF16_ROWS = 16
VMEM_LIMIT = 52 * 1024 * 1024
FFN_TM = 512
PROJ_TM = 512
MIX_TM = 512
IDX_TK = 512
ATT_TK = 512
NEAR_W = 2 * Q_BLOCK
BISECT_FIXED_STEPS = 14

INT_MIN = -2 ** 31


def _rms(x, g):
    return x * lax.rsqrt(jnp.mean(x * x, axis=-1, keepdims=True) + EPS) * g


def _params(*sem):
    return pltpu.CompilerParams(dimension_semantics=sem, vmem_limit_bytes=VMEM_LIMIT)


def _dot_nt(a, b):
    return lax.dot_general(a, b, (((1,), (1,)), ((), ())), preferred_element_type=F32)


def _ffn_kernel(x_ref, g_ref, wa_ref, wb_ref, wo_ref, fg_ref, o_ref, xn_ref, acc_ref, *, n_ff, final):
    j = pl.program_id(1)

    @pl.when(j == 0)
    def _():
        xn_ref[...] = _rms(x_ref[...], g_ref[...]).astype(BF16)
        acc_ref[...] = jnp.zeros_like(acc_ref)

    xn = xn_ref[...]
    a = jnp.dot(xn, wa_ref[...], preferred_element_type=F32)
    b = jnp.dot(xn, wb_ref[...], preferred_element_type=F32)
    act = (a * jax.nn.sigmoid(a) * b).astype(BF16)
    acc_ref[...] += jnp.dot(act, wo_ref[...], preferred_element_type=F32)

    @pl.when(j == n_ff - 1)
    def _():
        y = x_ref[...] + 0.5 * acc_ref[...]
        if final:
            y = _rms(y, fg_ref[...])
        o_ref[...] = y


def _ffn(h, g, w_in, w_out, fg, layer, *, final):
    T, D = h.shape
    F = w_out.shape[1]
    n_ff = 2 if F % (2 * LANES) == 0 else 1
    fc = F // n_ff
    tm = min(FFN_TM, T)
    return pl.pallas_call(
        functools.partial(_ffn_kernel, n_ff=n_ff, final=final),
        out_shape=jax.ShapeDtypeStruct((T, D), F32),
        grid=(T // tm, n_ff),
        in_specs=[
            pl.BlockSpec((tm, D), lambda i, j: (i, 0)),
            pl.BlockSpec((None, 1, D), lambda i, j: (layer, 0, 0)),
            pl.BlockSpec((None, D, fc), lambda i, j: (layer, 0, j)),
            pl.BlockSpec((None, D, fc), lambda i, j: (layer, 0, n_ff + j)),
            pl.BlockSpec((None, fc, D), lambda i, j: (layer, j, 0)),
            pl.BlockSpec((1, D), lambda i, j: (0, 0)),
        ],
        out_specs=pl.BlockSpec((tm, D), lambda i, j: (i, 0)),
        scratch_shapes=[pltpu.VMEM((tm, D), BF16), pltpu.VMEM((tm, D), F32)],
        compiler_params=_params("parallel", "arbitrary"),
        name="ffn",
    )(h, g, w_in, w_in, w_out, fg)


_PROJ_W = ("wq", "wukt", "wckv", "kvg", "wiqt", "wik", "wiwt", "wz", "wqc")


def _proj_kernel(h_ref, g_ref, wq_ref, wukt_ref, wckv_ref, kvg_ref, wiqt_ref, wik_ref, wiwt_ref, wz_ref, wqc_ref,
                 qlt_ref, ckv_ref, ckvt_ref, iqt_ref, ik_ref, iwt_ref, z_ref, qc_ref):
    u = _rms(h_ref[...], g_ref[...]).astype(BF16)
    q = jnp.dot(u, wq_ref[...], preferred_element_type=F32).astype(BF16)
    qlt_ref[...] = (_dot_nt(wukt_ref[...], q) * (A_HEAD_DIM ** -0.5 * LOG2E)).astype(BF16)
    c = _rms(jnp.dot(u, wckv_ref[...], preferred_element_type=F32), kvg_ref[...])
    ckv_ref[...] = c.astype(BF16)
    ckvt_ref[...] = c.T.astype(BF16)
    iqt_ref[...] = _dot_nt(wiqt_ref[...], u).astype(BF16)
    ik_ref[...] = jnp.dot(u, wik_ref[...], preferred_element_type=F32)[:, :IDX_DIM].astype(BF16)
    iwt = _dot_nt(wiwt_ref[...], u)[:IDX_HEADS]
    iwt_ref[...] = iwt * ((IDX_DIM ** -0.5) * (IDX_HEADS ** -0.5))
    z_ref[...] = jnp.dot(u, wz_ref[...], preferred_element_type=F32)
    qc = jnp.dot(u, wqc_ref[...], preferred_element_type=F32) * (C_HEAD_DIM ** -0.5)
    qc_ref[...] = qc.astype(BF16)


def _proj(h, g, wp, layer, B, S):
    T, D = h.shape
    tm = min(PROJ_TM, S)
    nt = S // tm
    RL = A_HEADS * KV_RANK
    QI = IDX_HEADS * IDX_DIM

    def wspec(arr):
        return pl.BlockSpec((None,) + arr.shape[1:], lambda b, i: (layer,) + (0,) * (arr.ndim - 1))

    tok = lambda w: pl.BlockSpec((tm, w), lambda b, i: (b * nt + i, 0))
    tokt = lambda w: pl.BlockSpec((None, w, tm), lambda b, i: (b, 0, i))
    return pl.pallas_call(
        _proj_kernel,
        out_shape=[
            jax.ShapeDtypeStruct((B, RL, S), BF16),
            jax.ShapeDtypeStruct((T, KV_RANK), BF16),
            jax.ShapeDtypeStruct((B, KV_RANK, S), BF16),
            jax.ShapeDtypeStruct((B, QI, S), BF16),
            jax.ShapeDtypeStruct((T, IDX_DIM), BF16),
            jax.ShapeDtypeStruct((B, IDX_HEADS, S), F32),
            jax.ShapeDtypeStruct((T, POOL_WIDTH), F32),
            jax.ShapeDtypeStruct((T, C_WIDTH), BF16),
        ],
        grid=(B, nt),
        in_specs=[tok(D), wspec(g)] + [wspec(wp[k]) for k in _PROJ_W],
        out_specs=[tokt(RL), tok(KV_RANK), tokt(KV_RANK), tokt(QI), tok(IDX_DIM), tokt(IDX_HEADS),
                   tok(POOL_WIDTH), tok(C_WIDTH)],
        compiler_params=_params("parallel", "parallel"),
        name="proj",
    )(h, g, *[wp[k] for k in _PROJ_W])


def _t5_bucket_np(dist):
    max_exact = REL_BUCKETS // 2
    n = np.maximum(dist, 0)
    nf = np.maximum(n, 1).astype(np.float32)
    large = max_exact + (np.log(nf / np.float32(max_exact)) / np.float32(math.log(REL_MAX_DIST / max_exact))
                         * np.float32(REL_BUCKETS - max_exact)).astype(np.int32)
    large = np.minimum(large, REL_BUCKETS - 1)
    return np.where(n < max_exact, n, large).astype(np.int32)


def _bias_kernel(rb_ref, bucket_ref, o_ref):
    for k in range(2):
        bk = bucket_ref[k]
        for h in range(A_HEADS):
            far = rb_ref[REL_BUCKETS - 1, h]
            acc = jnp.zeros(bk.shape, F32)
            for b in range(REL_BUCKETS - 1):
                acc = jnp.where(bk == b, (rb_ref[b, h] - far) * LOG2E, acc)
            o_ref[k, :, h * Q_BLOCK:(h + 1) * Q_BLOCK] = acc


def _bias_tables(rel_bias):
    c = np.arange(NEAR_W)[:, None]
    r = np.arange(Q_BLOCK)[None, :]
    buckets = np.stack([_t5_bucket_np(r + Q_BLOCK - c), _t5_bucket_np(r - c)])
    return pl.pallas_call(
        _bias_kernel,
        out_shape=jax.ShapeDtypeStruct((2, NEAR_W, A_HEADS * Q_BLOCK), F32),
        in_specs=[pl.BlockSpec(memory_space=pltpu.SMEM), pl.BlockSpec(memory_space=pltpu.VMEM)],
        out_specs=pl.BlockSpec(memory_space=pltpu.VMEM),
        name="bias_tables",
    )(rel_bias, jnp.asarray(buckets))


def _key_to_f32(k):
    return lax.bitcast_convert_type(jnp.where(k < 0, INT_MIN - k, k), F32)


def _f32_to_key(x):
    b = lax.bitcast_convert_type(x, jnp.int32)
    return jnp.where(b < 0, INT_MIN - b, b)


def _dsa_kernel(qlt_ref, iqt_ref, iwt_ref, ik_ref, ckv_ref, ckvt_ref, tbl_ref, wuvt_ref, o_ref,
                sc_ref, qs_ref, iqs_ref, m_ref, l_ref, acc_ref, accn_ref, fold_ref, *, k_sel, seq):
    tq = Q_BLOCK
    H = A_HEADS
    i = pl.program_id(1)
    row0 = i * tq
    n_idx = (row0 + tq + IDX_TK - 1) // IDX_TK

    eye_b = lax.broadcasted_iota(jnp.int32, (tq, tq), 0) == lax.broadcasted_iota(jnp.int32, (tq, tq), 1)
    for h in range(H):
        qs_ref[0:KV_RANK, h * tq:(h + 1) * tq] = qlt_ref[h * KV_RANK:(h + 1) * KV_RANK, :]
        iqs_ref[:, h * tq:(h + 1) * tq] = iqt_ref[h * IDX_DIM:(h + 1) * IDX_DIM, :]

    w_rows = [iwt_ref[h:h + 1, :] for h in range(IDX_HEADS)]
    qid = row0 + lax.broadcasted_iota(jnp.int32, (IDX_TK, tq), 1)
    kid0 = lax.broadcasted_iota(jnp.int32, (IDX_TK, tq), 0)
    fold_ref[...] = jnp.full(fold_ref.shape, -jnp.inf, F32)

    def idx_body(j, carry):
        r0 = pl.multiple_of(j * IDX_TK, IDX_TK)
        d = jnp.dot(ik_ref[pl.ds(r0, IDX_TK), :], iqs_ref[...], preferred_element_type=F32)
        s = jnp.maximum(d[:, 0:tq], 0.0) * w_rows[0]
        for h in range(1, IDX_HEADS):
            s = s + jnp.maximum(d[:, h * tq:(h + 1) * tq], 0.0) * w_rows[h]
        s = jnp.where(kid0 + r0 <= qid, s, -jnp.inf)
        sc_ref[pl.ds(r0, IDX_TK), :] = s
        f = s[0:k_sel]
        for q in range(1, IDX_TK // k_sel):
            f = jnp.maximum(f, s[q * k_sel:(q + 1) * k_sel])
        fold_ref[...] = jnp.maximum(fold_ref[...], f)
        return carry

    lax.fori_loop(0, n_idx, idx_body, 0)

    n_par = 8

    def count(pred):
        def body(j, acc):
            r0 = pl.multiple_of(j * IDX_TK, IDX_TK)
            ind = jnp.where(pred(sc_ref[pl.ds(r0, IDX_TK), :], r0), 1.0, 0.0)
            return acc + jnp.sum(ind.reshape(n_par, IDX_TK // n_par, tq), axis=0)
        acc = lax.fori_loop(0, n_idx, body, jnp.zeros((IDX_TK // n_par, tq), F32))
        return jnp.sum(acc, axis=0, keepdims=True)

    kf = float(k_sel)
    fold = fold_ref[...]
    lo0 = _f32_to_key(jnp.min(fold, axis=0, keepdims=True))
    hi0 = _f32_to_key(jnp.max(fold, axis=0, keepdims=True))

    def n_open(lo, hi):
        return jnp.sum((lo < hi).astype(F32))

    def bis_step(lo, hi):
        d = hi - lo
        mid = lo + lax.shift_right_logical(d, 1) + (d & 1)
        thr = _key_to_f32(mid)
        cnt = count(lambda s, r0: s >= thr)
        active = lo < hi
        ge = cnt >= kf
        lo_n = jnp.where(active & ge, mid, lo)
        hi_n = jnp.where(active, jnp.where(ge, jnp.where(cnt == kf, mid, hi), mid - 1), hi)
        return lo_n, hi_n

    def bis_body(st):
        lo_n, hi_n = bis_step(st[0], st[1])
        return lo_n, hi_n, n_open(lo_n, hi_n)

    lo, hi = lax.fori_loop(0, BISECT_FIXED_STEPS, lambda _, st: bis_step(*st), (lo0, hi0))
    lo, _, _ = lax.while_loop(lambda st: st[2] > 0.0, bis_body, (lo, hi, n_open(lo, hi)))
    thr = _key_to_f32(lo)

    cnt_ge = count(lambda s, r0: s >= thr)
    n_tie = jnp.sum((cnt_ge > kf).astype(F32))

    @pl.when(n_tie > 0.0)
    def _():
        need = kf - count(lambda s, r0: s > thr)

        def pos_body(_, st):
            plo, phi = st
            mid = (plo + phi) // 2
            c = count(lambda s, r0: (s == thr) & (kid0 + r0 <= mid))
            ok = c >= need
            return jnp.where(ok, plo, mid + 1), jnp.where(ok, mid, phi)

        plo0 = jnp.zeros((1, tq), jnp.int32)
        phi0 = jnp.full((1, tq), seq - 1, jnp.int32)
        last, _ = lax.fori_loop(0, max(1, (seq - 1).bit_length()), pos_body, (plo0, phi0))

        def drop_body(j, carry):
            r0 = pl.multiple_of(j * IDX_TK, IDX_TK)
            s = sc_ref[pl.ds(r0, IDX_TK), :]
            sc_ref[pl.ds(r0, IDX_TK), :] = jnp.where((s == thr) & (kid0 + r0 > last), -jnp.inf, s)
            return carry

        lax.fori_loop(0, n_idx, drop_body, 0)

    def set_shift_rows(neg_shift):
        for h in range(H):
            blk = jnp.where(eye_b, neg_shift[:, h * tq:(h + 1) * tq], 0.0)
            qs_ref[KV_RANK:, h * tq:(h + 1) * tq] = blk.astype(BF16)

    def qk(ckv_t, keep):
        lhs = jnp.concatenate([ckv_t, jnp.where(keep, 1.0, MASK_BIG).astype(BF16)], axis=1)
        return jnp.dot(lhs, qs_ref[...], preferred_element_type=F32)

    near0 = jnp.maximum(row0 - tq, 0)
    n_far = (near0 + ATT_TK - 1) // ATT_TK
    fkid0 = lax.broadcasted_iota(jnp.int32, (ATT_TK, tq), 0)
    n0 = pl.multiple_of(near0, tq)
    first = (i == 0).astype(jnp.int32)
    nkid = n0 + lax.broadcasted_iota(jnp.int32, (NEAR_W, tq), 0)
    nqid = row0 + lax.broadcasted_iota(jnp.int32, (NEAR_W, tq), 1)

    def far_tile(j):
        r0 = pl.multiple_of(j * ATT_TK, ATT_TK)
        keep = (sc_ref[pl.ds(r0, ATT_TK), :] >= thr) & (fkid0 + r0 < near0)
        return ckv_ref[pl.ds(r0, ATT_TK), :], ckvt_ref[:, pl.ds(r0, ATT_TK)], keep

    def near_tile():
        keep = (sc_ref[pl.ds(n0, NEAR_W), :] >= thr) & (nkid <= nqid)
        return ckv_ref[pl.ds(n0, NEAR_W), :], ckvt_ref[:, pl.ds(n0, NEAR_W)], keep

    set_shift_rows(jnp.full((1, H * tq), -1.0, F32))
    ckv_n, ckvt_n, keep_n = near_tile()
    lg = qk(ckv_n, keep_n) + tbl_ref[first]
    m_n = jnp.maximum(jnp.max(lg, axis=0, keepdims=True), M_INIT)
    p = jnp.exp2(lg - m_n)
    l_n = jnp.sum(p, axis=0, keepdims=True)
    accn_ref[...] = jnp.dot(ckvt_n, p.astype(BF16), preferred_element_type=F32)
    s_n = m_n + 1.0

    s_f = jnp.clip(s_n, 1.0, SHIFT_MAX).astype(BF16).astype(F32)
    set_shift_rows(-s_f)
    acc_ref[...] = jnp.zeros(acc_ref.shape, F32)

    def far_body(j, l_f):
        ckv_t, ckvt_t, keep = far_tile(j)
        p = jnp.exp2(qk(ckv_t, keep))
        acc_ref[...] += jnp.dot(ckvt_t, p.astype(BF16), preferred_element_type=F32)
        return l_f + jnp.sum(p, axis=0, keepdims=True)

    l_f = lax.fori_loop(0, n_far, far_body, jnp.zeros((1, H * tq), F32))

    has_far = n_far > 0
    s_max = jnp.maximum(s_n, s_f)
    w_n = jnp.where(has_far, jnp.exp2(s_n - s_max), 1.0)
    w_f = jnp.where(has_far, jnp.exp2(s_f - s_max), 0.0)
    l_tot = l_n * w_n + l_f * w_f
    acc_ref[...] = (accn_ref[...] * w_n + acc_ref[...] * w_f) * (1.0 / l_tot)
    in_range = (l_tot > L_TOT_MIN) & (l_tot < L_TOT_MAX)
    n_bad = jnp.sum(jnp.where(in_range, 0.0, 1.0))

    @pl.when(n_bad > 0.0)
    def _():
        set_shift_rows(jnp.full((1, H * tq), -1.0, F32))
        m_ref[...] = jnp.full(m_ref.shape, M_INIT, F32)
        l_ref[...] = jnp.zeros(l_ref.shape, F32)
        acc_ref[...] = jnp.zeros(acc_ref.shape, F32)

        def attend(ckv_t, ckvt_t, keep, bias):
            lg = qk(ckv_t, keep)
            if bias is not None:
                lg = lg + bias
            m_prev = m_ref[...]
            m_new = jnp.maximum(m_prev, jnp.max(lg, axis=0, keepdims=True))
            alpha = jnp.exp2(m_prev - m_new)
            p = jnp.exp2(lg - m_new)
            l_ref[...] = alpha * l_ref[...] + jnp.sum(p, axis=0, keepdims=True)
            m_ref[...] = m_new
            pv = jnp.dot(ckvt_t, p.astype(BF16), preferred_element_type=F32)
            acc_ref[...] = alpha * acc_ref[...] + pv

        def slow_body(j, carry):
            attend(*far_tile(j), None)
            return carry

        lax.fori_loop(0, n_far, slow_body, 0)
        attend(*near_tile(), tbl_ref[first])
        acc_ref[...] = acc_ref[...] * (1.0 / l_ref[...])

    ot = acc_ref[...].astype(BF16)
    o_st = jnp.concatenate([ot[:, h * tq:(h + 1) * tq] for h in range(H)], axis=0)
    yt = jnp.dot(wuvt_ref[...], o_st, preferred_element_type=F32)
    o_ref[...] = yt.T.astype(BF16)


def _dsa(qlt, iqt, iwt, ik, ckv, ckvt, tbl, wuvt, layer, B, S):
    tq = Q_BLOCK
    nb = S // tq
    RL = A_HEADS * KV_RANK
    QI = IDX_HEADS * IDX_DIM
    k_sel = min(TOPK_MAX, S // 4)
    assert IDX_TK % k_sel == 0 and k_sel % SUBLANES == 0
    return pl.pallas_call(
        functools.partial(_dsa_kernel, k_sel=k_sel, seq=S),
        out_shape=jax.ShapeDtypeStruct((B * S, A_WIDTH), BF16),
        grid=(B, nb),
        in_specs=[
            pl.BlockSpec((None, RL, tq), lambda b, i: (b, 0, i)),
            pl.BlockSpec((None, QI, tq), lambda b, i: (b, 0, i)),
            pl.BlockSpec((None, IDX_HEADS, tq), lambda b, i: (b, 0, i)),
            pl.BlockSpec((None, S, IDX_DIM), lambda b, i: (b, 0, 0)),
            pl.BlockSpec((None, S, KV_RANK), lambda b, i: (b, 0, 0)),
            pl.BlockSpec((None, KV_RANK, S), lambda b, i: (b, 0, 0)),
            pl.BlockSpec((2, NEAR_W, A_HEADS * tq), lambda b, i: (0, 0, 0)),
            pl.BlockSpec((None, A_WIDTH, RL), lambda b, i: (layer, 0, 0)),
        ],
        out_specs=pl.BlockSpec((tq, A_WIDTH), lambda b, i: (b * nb + i, 0)),
        scratch_shapes=[
            pltpu.VMEM((S, tq), F32),
            pltpu.VMEM((KV_RANK + tq, A_HEADS * tq), BF16),
            pltpu.VMEM((IDX_DIM, IDX_HEADS * tq), BF16),
            pltpu.VMEM((1, A_HEADS * tq), F32),
            pltpu.VMEM((1, A_HEADS * tq), F32),
            pltpu.VMEM((KV_RANK, A_HEADS * tq), F32),
            pltpu.VMEM((KV_RANK, A_HEADS * tq), F32),
            pltpu.VMEM((k_sel, tq), F32),
        ],
        compiler_params=_params("parallel", "arbitrary"),
        name="dsa",
    )(qlt, iqt, iwt, ik, ckv, ckvt, tbl, wuvt)


def _memkv_kernel(mem_ref, g_ref, w_ref, kbd_ref, vbd_ref):
    M = mem_ref.shape[0]
    mn = _rms(mem_ref[...], g_ref[...]).astype(BF16)
    kv = jnp.dot(mn, w_ref[...], preferred_element_type=F32)
    kt = kv[:, :C_WIDTH].T
    v = kv[:, C_WIDTH:]
    r = lax.broadcasted_iota(jnp.int32, (C_WIDTH, C_HEADS * M), 0) // C_HEAD_DIM
    c = lax.broadcasted_iota(jnp.int32, (C_WIDTH, C_HEADS * M), 1) // M
    kbd_ref[...] = jnp.where(r == c, jnp.concatenate([kt] * C_HEADS, axis=1), 0.0).astype(BF16)
    r = lax.broadcasted_iota(jnp.int32, (C_HEADS * M, C_WIDTH), 0) // M
    c = lax.broadcasted_iota(jnp.int32, (C_HEADS * M, C_WIDTH), 1) // C_HEAD_DIM
    vbd_ref[...] = jnp.where(r == c, jnp.concatenate([v] * C_HEADS, axis=0), 0.0).astype(BF16)


def _memkv(mem, g, w, layer):
    B, M, D = mem.shape
    return pl.pallas_call(
        _memkv_kernel,
        out_shape=[jax.ShapeDtypeStruct((B, C_WIDTH, C_HEADS * M), BF16),
                   jax.ShapeDtypeStruct((B, C_HEADS * M, C_WIDTH), BF16)],
        grid=(B,),
        in_specs=[pl.BlockSpec((None, M, D), lambda b: (b, 0, 0)),
                  pl.BlockSpec((None, 1, D), lambda b: (layer, 0, 0)),
                  pl.BlockSpec((None, D, 2 * C_WIDTH), lambda b: (layer, 0, 0))],
        out_specs=[pl.BlockSpec((None, C_WIDTH, C_HEADS * M), lambda b: (b, 0, 0)),
                   pl.BlockSpec((None, C_HEADS * M, C_WIDTH), lambda b: (b, 0, 0))],
        compiler_params=_params("parallel"),
        name="memkv",
    )(mem, g, w)


def _mix_kernel(h_ref, g_ref, wg_ref, ya_ref, z_ref, zh_ref, qc_ref, kbd_ref, vbd_ref, wpool_ref, ps_ref,
                wa_ref, wb_ref, wc_ref, wo_ref, o_ref, zx_ref):
    tm, D = h_ref.shape
    i = pl.program_id(1)
    M = kbd_ref.shape[1] // C_HEADS
    h = h_ref[...]
    u = _rms(h, g_ref[...]).astype(BF16)

    z = z_ref[...]
    zx_ref[0:POOL_HALO, :] = jnp.where(i == 0, 0.0, zh_ref[...])
    zx_ref[POOL_HALO:, :] = z
    pos = i * tm + lax.broadcasted_iota(jnp.int32, (tm, 1), 0) + 1
    grp = lax.broadcasted_iota(jnp.int32, (1, POOL_WIDTH), 1) // POOL_GROUP
    win = z
    pooled = jnp.zeros_like(z)
    k = 1
    for gi, w in enumerate(POOL_WINDOWS):
        while k < w:
            win = win + zx_ref[POOL_HALO - k:POOL_HALO - k + tm, :]
            k += 1
        cnt = jnp.minimum(pos, w).astype(F32)
        pooled = jnp.where(grp == gi, win / cnt - z, pooled)
    yb = jnp.dot(pooled.astype(BF16), wpool_ref[...], preferred_element_type=F32) * ps_ref[...]

    lg = jnp.dot(qc_ref[...], kbd_ref[...], preferred_element_type=F32)
    ps = []
    for hc in range(C_HEADS):
        seg = lg[:, hc * M:(hc + 1) * M]
        e = jnp.exp(seg - jnp.max(seg, axis=-1, keepdims=True))
        ps.append(e / jnp.sum(e, axis=-1, keepdims=True))
    p = jnp.concatenate(ps, axis=1).astype(BF16)
    yc = jnp.dot(p, vbd_ref[...], preferred_element_type=F32)

    def gate(n):
        pre = jnp.dot(u, wg_ref[:, n * D:(n + 1) * D], preferred_element_type=F32)
        return jax.nn.sigmoid(pre)

    merged = gate(0) * jnp.dot(ya_ref[...], wa_ref[...], preferred_element_type=F32)
    merged += gate(1) * jnp.dot(yb.astype(BF16), wb_ref[...], preferred_element_type=F32)
    merged += gate(2) * jnp.dot(yc.astype(BF16), wc_ref[...], preferred_element_type=F32)
    o_ref[...] = h + jnp.dot(merged.astype(BF16), wo_ref[...], preferred_element_type=F32)


def _mix(h, g, ya, z, qc, kbd, vbd, wp, layer, B, S):
    T, D = h.shape
    tm = min(MIX_TM, S)
    nt = S // tm
    hb = tm // POOL_HALO

    def wspec(arr):
        return pl.BlockSpec((None,) + arr.shape[1:], lambda b, i: (layer,) + (0,) * (arr.ndim - 1))

    tok = lambda w: pl.BlockSpec((tm, w), lambda b, i: (b * nt + i, 0))
    names = ("wpool", "ps", "wa", "wb", "wc", "wo")
    return pl.pallas_call(
        _mix_kernel,
        out_shape=jax.ShapeDtypeStruct((T, D), F32),
        grid=(B, nt),
        in_specs=[tok(D), wspec(g), wspec(wp["wg"]), tok(A_WIDTH), tok(POOL_WIDTH),
                  pl.BlockSpec((POOL_HALO, POOL_WIDTH), lambda b, i: (jnp.maximum((b * nt + i) * hb - 1, 0), 0)),
                  tok(C_WIDTH),
                  pl.BlockSpec((None,) + kbd.shape[1:], lambda b, i: (b, 0, 0)),
                  pl.BlockSpec((None,) + vbd.shape[1:], lambda b, i: (b, 0, 0))]
                 + [wspec(wp[k]) for k in names],
        out_specs=tok(D),
        scratch_shapes=[pltpu.VMEM((tm + POOL_HALO, POOL_WIDTH), F32)],
        compiler_params=_params("parallel", "parallel"),
        name="mix",
    )(h, g, wp["wg"], ya, z, z, qc, kbd, vbd, *[wp[k] for k in names])


def _block_diag(blocks):
    L, G, r, c = blocks.shape
    eye = jnp.eye(G, dtype=blocks.dtype)
    return jnp.einsum("lgrc,gh->lgrhc", blocks, eye).reshape(L, G * r, G * c)


def _prep_weights(w_in, kv_norm, w_uk, w_uv, w_pool, pool_scale, w_branch_a, w_branch_b, w_branch_c, w_out):
    L, D, _ = w_in.shape
    sizes = (A_WIDTH, KV_RANK, IDX_HEADS * IDX_DIM, IDX_DIM, IDX_HEADS, POOL_WIDTH, C_WIDTH, N_BRANCH * D)
    offs = np.concatenate([[0], np.cumsum(sizes)])
    wq, wckv, wiq, wik, wiw, wz, wqc, wg = [w_in[:, :, offs[n]:offs[n + 1]] for n in range(len(sizes))]
    proj = dict(
        wq=wq.astype(BF16),
        wukt=_block_diag(jnp.transpose(w_uk, (0, 2, 1, 3))).astype(BF16),
        wckv=wckv.astype(BF16),
        kvg=kv_norm[:, None, :],
        wiqt=jnp.transpose(wiq, (0, 2, 1)).astype(BF16),
        wik=jnp.pad(wik, ((0, 0), (0, 0), (0, LANES - IDX_DIM))).astype(BF16),
        wiwt=jnp.pad(jnp.transpose(wiw, (0, 2, 1)), ((0, 0), (0, BF16_ROWS - IDX_HEADS), (0, 0))).astype(BF16),
        wz=wz.astype(BF16),
        wqc=wqc.astype(BF16),
    )
    wuvt = _block_diag(jnp.transpose(w_uv, (0, 2, 3, 1))).astype(BF16)
    mix = dict(
        wg=wg.astype(BF16),
        wpool=_block_diag(w_pool).astype(BF16),
        ps=pool_scale[:, None, :],
        wa=w_branch_a.astype(BF16), wb=w_branch_b.astype(BF16), wc=w_branch_c.astype(BF16),
        wo=w_out.astype(BF16),
    )
    return proj, wuvt, mix


def kernel(x, mem, rel_bias, ffn1_norm, ffn1_w_in, ffn1_w_out, mix_norm, w_in, kv_norm, w_uk, w_uv, w_pool,
           pool_scale, mem_norm, w_mem_kv, w_branch_a, w_branch_b, w_branch_c, w_out, ffn2_norm, ffn2_w_in,
           ffn2_w_out, final_norm):
    B, S, D = x.shape
    L = w_in.shape[0]
    assert S % Q_BLOCK == 0 and S % IDX_TK == 0 and S % ATT_TK == 0 and S >= NEAR_W

    proj_w, wuvt, mix_w = _prep_weights(w_in, kv_norm, w_uk, w_uv, w_pool, pool_scale,
                                        w_branch_a, w_branch_b, w_branch_c, w_out)
    f1_in, f1_out = ffn1_w_in.astype(BF16), ffn1_w_out.astype(BF16)
    f2_in, f2_out = ffn2_w_in.astype(BF16), ffn2_w_out.astype(BF16)
    f1_g, f2_g, mx_g, mm_g = (a[:, None, :] for a in (ffn1_norm, ffn2_norm, mix_norm, mem_norm))
    fg = final_norm[None, :]
    wmem = w_mem_kv.astype(BF16)
    tbl = _bias_tables(rel_bias)

    h = x.reshape(B * S, D)
    for l in range(L):
        h = _ffn(h, f1_g, f1_in, f1_out, fg, l, final=False)
        qlt, ckv, ckvt, iqt, ik, iwt, z, qc = _proj(h, mx_g, proj_w, l, B, S)
        ya = _dsa(qlt, iqt, iwt, ik.reshape(B, S, IDX_DIM), ckv.reshape(B, S, KV_RANK), ckvt, tbl, wuvt, l, B, S)
        kbd, vbd = _memkv(mem, mm_g, wmem, l)
        h = _mix(h, mx_g, ya, z, qc, kbd, vbd, mix_w, l, B, S)
        h = _ffn(h, f2_g, f2_in, f2_out, fg, l, final=(l == L - 1))
    return h.reshape(B, S, D)
```

```python
import functools
import math

import numpy as np
import jax
import jax.numpy as jnp
from jax import lax
from jax.experimental import pallas as pl
from jax.experimental.pallas import tpu as pltpu

F32 = jnp.float32
BF16 = jnp.bfloat16

A_HEADS = 8
A_HEAD_DIM = 64
A_WIDTH = A_HEADS * A_HEAD_DIM
KV_RANK = 128
IDX_HEADS = 8
IDX_DIM = 32
TOPK_MAX = 256
Q_BLOCK = 128
POOL_WINDOWS = (2, 4, 8, 16)
POOL_GROUP = 64
POOL_WIDTH = len(POOL_WINDOWS) * POOL_GROUP
POOL_HALO = max(POOL_WINDOWS)
C_HEADS = 4
C_HEAD_DIM = 64
C_WIDTH = C_HEADS * C_HEAD_DIM
N_BRANCH = 3
REL_BUCKETS = 32
REL_MAX_DIST = 128
EPS = 1e-6
NEG = -1e30
M_INIT = -1e29
LOG2E = math.log2(math.e)
MASK_BIG = -NEG
SHIFT_MAX = 2.0 ** 20
L_TOT_MIN = 2.0 ** -60
L_TOT_MAX = 2.0 ** 100

LANES = 128
SUBLANES = 8
BF16_ROWS = 16
VMEM_LIMIT = 52 * 1024 * 1024
FFN_TM = 512
PROJ_TM = 512
MIX_TM = 512
IDX_TK = 512
ATT_TK = 512
NEAR_W = 2 * Q_BLOCK
COUNT_PARTIALS = 8
SORT_N = 16
LAYER_ROWS = IDX_TK // SORT_N
LAYER_FULL_STEPS = 4
BISECT_FIXED_STEPS = 12
BISECT_STEPS_PER_TEST = 4


def _oddeven_merge_sort(n):
    def merge(lo, hi, r):
        step = r * 2
        if step < hi - lo:
            yield from merge(lo, hi, step)
            yield from merge(lo + r, hi, step)
            yield from ((i, i + r) for i in range(lo + r, hi - r, step))
        else:
            yield (lo, lo + r)

    def sort(lo, hi):
        if hi - lo >= 1:
            mid = lo + (hi - lo) // 2
            yield from sort(lo, mid)
            yield from sort(mid + 1, hi)
            yield from merge(lo, hi, 1)

    return tuple(sort(0, n - 1))


_SORT_NETWORK = _oddeven_merge_sort(SORT_N)

INT_MIN = -2 ** 31


def _rms(x, g):
    return x * lax.rsqrt(jnp.mean(x * x, axis=-1, keepdims=True) + EPS) * g


def _params(*sem):
    return pltpu.CompilerParams(dimension_semantics=sem, vmem_limit_bytes=VMEM_LIMIT)


def _dot_nt(a, b):
    return lax.dot_general(a, b, (((1,), (1,)), ((), ())), preferred_element_type=F32)


def _ffn_kernel(x_ref, g_ref, wa_ref, wb_ref, wo_ref, fg_ref, o_ref, xn_ref, acc_ref, *, n_ff, final):
    j = pl.program_id(1)

    @pl.when(j == 0)
    def _():
        xn_ref[...] = _rms(x_ref[...], g_ref[...]).astype(BF16)
        acc_ref[...] = jnp.zeros_like(acc_ref)

    xn = xn_ref[...]
    a = jnp.dot(xn, wa_ref[...], preferred_element_type=F32)
    b = jnp.dot(xn, wb_ref[...], preferred_element_type=F32)
    act = (a * jax.nn.sigmoid(a) * b).astype(BF16)
    acc_ref[...] += jnp.dot(act, wo_ref[...], preferred_element_type=F32)

    @pl.when(j == n_ff - 1)
    def _():
        y = x_ref[...] + 0.5 * acc_ref[...]
        if final:
            y = _rms(y, fg_ref[...])
        o_ref[...] = y


def _ffn(h, g, w_in, w_out, fg, layer, *, final):
    T, D = h.shape
    F = w_out.shape[1]
    n_ff = 2 if F % (2 * LANES) == 0 else 1
    fc = F // n_ff
    tm = min(FFN_TM, T)
    return pl.pallas_call(
        functools.partial(_ffn_kernel, n_ff=n_ff, final=final),
        out_shape=jax.ShapeDtypeStruct((T, D), F32),
        grid=(T // tm, n_ff),
        in_specs=[
            pl.BlockSpec((tm, D), lambda i, j: (i, 0)),
            pl.BlockSpec((None, 1, D), lambda i, j: (layer, 0, 0)),
            pl.BlockSpec((None, D, fc), lambda i, j: (layer, 0, j)),
            pl.BlockSpec((None, D, fc), lambda i, j: (layer, 0, n_ff + j)),
            pl.BlockSpec((None, fc, D), lambda i, j: (layer, j, 0)),
            pl.BlockSpec((1, D), lambda i, j: (0, 0)),
        ],
        out_specs=pl.BlockSpec((tm, D), lambda i, j: (i, 0)),
        scratch_shapes=[pltpu.VMEM((tm, D), BF16), pltpu.VMEM((tm, D), F32)],
        compiler_params=_params("parallel", "arbitrary"),
        name="ffn",
    )(h, g, w_in, w_in, w_out, fg)


_PROJ_W = ("wq", "wukt", "wckv", "kvg", "wiqt", "wik", "wiwt", "wz", "wqc")


def _proj_kernel(h_ref, g_ref, wq_ref, wukt_ref, wckv_ref, kvg_ref, wiqt_ref, wik_ref, wiwt_ref, wz_ref, wqc_ref,
                 qlt_ref, ckv_ref, ckvt_ref, iqt_ref, ik_ref, iwt_ref, z_ref, qc_ref):
    u = _rms(h_ref[...], g_ref[...]).astype(BF16)
    q = jnp.dot(u, wq_ref[...], preferred_element_type=F32).astype(BF16)
    qlt_ref[...] = (_dot_nt(wukt_ref[...], q) * (A_HEAD_DIM ** -0.5 * LOG2E)).astype(BF16)
    c = _rms(jnp.dot(u, wckv_ref[...], preferred_element_type=F32), kvg_ref[...])
    ckv_ref[...] = c.astype(BF16)
    ckvt_ref[...] = c.T.astype(BF16)
    iqt_ref[...] = _dot_nt(wiqt_ref[...], u).astype(BF16)
    ik_ref[...] = jnp.dot(u, wik_ref[...], preferred_element_type=F32)[:, :IDX_DIM].astype(BF16)
    iwt = _dot_nt(wiwt_ref[...], u)[:IDX_HEADS]
    iwt_ref[...] = iwt * ((IDX_DIM ** -0.5) * (IDX_HEADS ** -0.5))
    z_ref[...] = jnp.dot(u, wz_ref[...], preferred_element_type=F32)
    qc = jnp.dot(u, wqc_ref[...], preferred_element_type=F32) * (C_HEAD_DIM ** -0.5)
    qc_ref[...] = qc.astype(BF16)


def _proj(h, g, wp, layer, B, S):
    T, D = h.shape
    tm = min(PROJ_TM, S)
    nt = S // tm
    RL = A_HEADS * KV_RANK
    QI = IDX_HEADS * IDX_DIM

    def wspec(arr):
        return pl.BlockSpec((None,) + arr.shape[1:], lambda b, i: (layer,) + (0,) * (arr.ndim - 1))

    tok = lambda w: pl.BlockSpec((tm, w), lambda b, i: (b * nt + i, 0))
    tokt = lambda w: pl.BlockSpec((None, w, tm), lambda b, i: (b, 0, i))
    return pl.pallas_call(
        _proj_kernel,
        out_shape=[
            jax.ShapeDtypeStruct((B, RL, S), BF16),
            jax.ShapeDtypeStruct((T, KV_RANK), BF16),
            jax.ShapeDtypeStruct((B, KV_RANK, S), BF16),
            jax.ShapeDtypeStruct((B, QI, S), BF16),
            jax.ShapeDtypeStruct((T, IDX_DIM), BF16),
            jax.ShapeDtypeStruct((B, IDX_HEADS, S), F32),
            jax.ShapeDtypeStruct((T, POOL_WIDTH), F32),
            jax.ShapeDtypeStruct((T, C_WIDTH), BF16),
        ],
        grid=(B, nt),
        in_specs=[tok(D), wspec(g)] + [wspec(wp[k]) for k in _PROJ_W],
        out_specs=[tokt(RL), tok(KV_RANK), tokt(KV_RANK), tokt(QI), tok(IDX_DIM), tokt(IDX_HEADS),
                   tok(POOL_WIDTH), tok(C_WIDTH)],
        compiler_params=_params("parallel", "parallel"),
        name="proj",
    )(h, g, *[wp[k] for k in _PROJ_W])


def _t5_bucket_np(dist):
    max_exact = REL_BUCKETS // 2
    n = np.maximum(dist, 0)
    nf = np.maximum(n, 1).astype(np.float32)
    large = max_exact + (np.log(nf / np.float32(max_exact)) / np.float32(math.log(REL_MAX_DIST / max_exact))
                         * np.float32(REL_BUCKETS - max_exact)).astype(np.int32)
    large = np.minimum(large, REL_BUCKETS - 1)
    return np.where(n < max_exact, n, large).astype(np.int32)


def _bias_kernel(rb_ref, bucket_ref, o_ref):
    for k in range(2):
        bk = bucket_ref[k]
        for h in range(A_HEADS):
            far = rb_ref[REL_BUCKETS - 1, h]
            acc = jnp.zeros(bk.shape, F32)
            for b in range(REL_BUCKETS - 1):
                acc = jnp.where(bk == b, (rb_ref[b, h] - far) * LOG2E, acc)
            o_ref[k, :, h * Q_BLOCK:(h + 1) * Q_BLOCK] = acc


def _bias_tables(rel_bias):
    c = np.arange(NEAR_W)[:, None]
    r = np.arange(Q_BLOCK)[None, :]
    buckets = np.stack([_t5_bucket_np(r + Q_BLOCK - c), _t5_bucket_np(r - c)])
    return pl.pallas_call(
        _bias_kernel,
        out_shape=jax.ShapeDtypeStruct((2, NEAR_W, A_HEADS * Q_BLOCK), F32),
        in_specs=[pl.BlockSpec(memory_space=pltpu.SMEM), pl.BlockSpec(memory_space=pltpu.VMEM)],
        out_specs=pl.BlockSpec(memory_space=pltpu.VMEM),
        name="bias_tables",
    )(rel_bias, jnp.asarray(buckets))


def _key_to_f32(k):
    return lax.bitcast_convert_type(jnp.where(k < 0, INT_MIN - k, k), F32)


def _f32_to_key(x):
    b = lax.bitcast_convert_type(x, jnp.int32)
    return jnp.where(b < 0, INT_MIN - b, b)


def _dsa_kernel(qlt_ref, iqt_ref, iwt_ref, ik_ref, ckv_ref, ckvt_ref, tbl_ref, wuvt_ref, o_ref,
                sc_ref, lay_ref, lmax_ref, thr_ref, qs_ref, iqs_ref, m_ref, l_ref, acc_ref, accn_ref, fold_ref,
                *, k_sel, seq):
    tq = Q_BLOCK
    H = A_HEADS
    i = pl.program_id(1)
    row0 = i * tq
    n_idx = (row0 + tq + IDX_TK - 1) // IDX_TK

    eye_b = lax.broadcasted_iota(jnp.int32, (tq, tq), 0) == lax.broadcasted_iota(jnp.int32, (tq, tq), 1)
    for h in range(H):
        qs_ref[0:KV_RANK, h * tq:(h + 1) * tq] = qlt_ref[h * KV_RANK:(h + 1) * KV_RANK, :]
        iqs_ref[:, h * tq:(h + 1) * tq] = iqt_ref[h * IDX_DIM:(h + 1) * IDX_DIM, :]

    w_rows = [iwt_ref[h:h + 1, :] for h in range(IDX_HEADS)]
    qid = row0 + lax.broadcasted_iota(jnp.int32, (IDX_TK, tq), 1)
    kid0 = lax.broadcasted_iota(jnp.int32, (IDX_TK, tq), 0)
    fold_ref[...] = jnp.full(fold_ref.shape, -jnp.inf, F32)
    lmax_ref[...] = jnp.full(lmax_ref.shape, -jnp.inf, F32)

    def idx_body(j, carry):
        r0 = pl.multiple_of(j * IDX_TK, IDX_TK)
        d = jnp.dot(ik_ref[pl.ds(r0, IDX_TK), :], iqs_ref[...], preferred_element_type=F32)
        s = jnp.maximum(d[:, 0:tq], 0.0) * w_rows[0]
        for h in range(1, IDX_HEADS):
            s = s + jnp.maximum(d[:, h * tq:(h + 1) * tq], 0.0) * w_rows[h]
        s = jnp.where(kid0 + r0 <= qid, s, -jnp.inf)
        sc_ref[pl.ds(r0, IDX_TK), :] = s
        f = s[0:k_sel]
        for q in range(1, IDX_TK // k_sel):
            f = jnp.maximum(f, s[q * k_sel:(q + 1) * k_sel])
        fold_ref[...] = jnp.maximum(fold_ref[...], f)
        n_grp = IDX_TK // (SORT_N * SUBLANES)
        x = [jnp.concatenate([s[(g * SORT_N + r) * SUBLANES:(g * SORT_N + r + 1) * SUBLANES]
                              for g in range(n_grp)], axis=0) for r in range(SORT_N)]
        for a, b in _SORT_NETWORK:
            x[a], x[b] = jnp.maximum(x[a], x[b]), jnp.minimum(x[a], x[b])
        for r in range(SORT_N):
            lay_ref[pl.ds(r0 + r * LAYER_ROWS, LAYER_ROWS), :] = x[r]
            m = x[r][0:SUBLANES]
            for g in range(1, n_grp):
                m = jnp.maximum(m, x[r][g * SUBLANES:(g + 1) * SUBLANES])
            lmax_ref[r * SUBLANES:(r + 1) * SUBLANES, :] = jnp.maximum(lmax_ref[r * SUBLANES:(r + 1) * SUBLANES, :], m)
        return carry

    lax.fori_loop(0, n_idx, idx_body, 0)

    n_par = COUNT_PARTIALS

    def count(pred):
        def body(j, acc):
            r0 = pl.multiple_of(j * IDX_TK, IDX_TK)
            ind = jnp.where(pred(sc_ref[pl.ds(r0, IDX_TK), :], r0), 1.0, 0.0)
            return acc + jnp.sum(ind.reshape(n_par, IDX_TK // n_par, tq), axis=0)
        acc = lax.fori_loop(0, n_idx, body, jnp.zeros((IDX_TK // n_par, tq), F32))
        return jnp.sum(acc, axis=0, keepdims=True)

    kf = float(k_sel)
    fold = fold_ref[...]
    lo0 = _f32_to_key(jnp.min(fold, axis=0, keepdims=True))
    hi0 = _f32_to_key(jnp.max(fold, axis=0, keepdims=True))

    def n_open(lo, hi):
        return jnp.sum((lo < hi).astype(F32))

    def run_bisection(step, lo, hi, fixed_steps):
        lo, hi = lax.fori_loop(0, fixed_steps, lambda _, st: step(*st), (lo, hi))

        def body(st):
            lo_n, hi_n = lax.fori_loop(0, BISECT_STEPS_PER_TEST, lambda _, s2: step(*s2), (st[0], st[1]))
            return lo_n, hi_n, n_open(lo_n, hi_n)

        lo, hi, _ = lax.while_loop(lambda st: st[2] > 0.0, body, (lo, hi, n_open(lo, hi)))
        return lo

    def bis_step(count_fn, lo, hi):
        d = hi - lo
        mid = lo + lax.shift_right_logical(d, 1) + (d & 1)
        thr = _key_to_f32(mid)
        cnt = count_fn(lambda s, r0: s >= thr)
        active = lo < hi
        ge = cnt >= kf
        lo_n = jnp.where(active & ge, mid, lo)
        hi_n = jnp.where(active, jnp.where(ge, jnp.where(cnt == kf, mid, hi), mid - 1), hi)
        return lo_n, hi_n

    lo, hi = lax.fori_loop(0, LAYER_FULL_STEPS, lambda _, st: bis_step(count, *st), (lo0, hi0))
    layer_max = [jnp.max(lmax_ref[r * SUBLANES:(r + 1) * SUBLANES, :], axis=0, keepdims=True)
                 for r in range(SORT_N)]
    lo_f = _key_to_f32(lo)
    n_alive = sum(jnp.max(jnp.where(layer_max[r] >= lo_f, 1.0, 0.0)) for r in range(SORT_N))
    n_quarters = (n_alive.astype(jnp.int32) + SORT_N // 4 - 1) // (SORT_N // 4)

    for k in range(1, 5):
        rows = k * (SORT_N // 4) * LAYER_ROWS

        def count_top(pred, rows=rows):
            def body(j, acc):
                r0 = pl.multiple_of(j * IDX_TK, IDX_TK)
                ind = jnp.where(pred(lay_ref[pl.ds(r0, rows), :], r0), 1.0, 0.0)
                return acc + jnp.sum(ind.reshape(n_par, rows // n_par, tq), axis=0)
            acc = lax.fori_loop(0, n_idx, body, jnp.zeros((rows // n_par, tq), F32))
            return jnp.sum(acc, axis=0, keepdims=True)

        cond = (n_quarters <= 1) if k == 1 else ((n_quarters >= 4) if k == 4 else (n_quarters == k))

        @pl.when(cond)
        def _(count_top=count_top):
            thr_ref[...] = run_bisection(functools.partial(bis_step, count_top), lo, hi, BISECT_FIXED_STEPS)

    lo = thr_ref[...]
    thr = _key_to_f32(lo)

    cnt_ge = count(lambda s, r0: s >= thr)
    n_tie = jnp.sum((cnt_ge > kf).astype(F32))

    @pl.when(n_tie > 0.0)
    def _():
        need = kf - count(lambda s, r0: s > thr)

        def pos_body(_, st):
            plo, phi = st
            mid = (plo + phi) // 2
            c = count(lambda s, r0: (s == thr) & (kid0 + r0 <= mid))
            ok = c >= need
            return jnp.where(ok, plo, mid + 1), jnp.where(ok, mid, phi)

        plo0 = jnp.zeros((1, tq), jnp.int32)
        phi0 = jnp.full((1, tq), seq - 1, jnp.int32)
        last, _ = lax.fori_loop(0, max(1, (seq - 1).bit_length()), pos_body, (plo0, phi0))

        def drop_body(j, carry):
            r0 = pl.multiple_of(j * IDX_TK, IDX_TK)
            s = sc_ref[pl.ds(r0, IDX_TK), :]
            sc_ref[pl.ds(r0, IDX_TK), :] = jnp.where((s == thr) & (kid0 + r0 > last), -jnp.inf, s)
            return carry

        lax.fori_loop(0, n_idx, drop_body, 0)

    def set_shift_rows(neg_shift):
        for h in range(H):
            blk = jnp.where(eye_b, neg_shift[:, h * tq:(h + 1) * tq], 0.0)
            qs_ref[KV_RANK:, h * tq:(h + 1) * tq] = blk.astype(BF16)

    def qk(ckv_t, keep):
        lhs = jnp.concatenate([ckv_t, jnp.where(keep, 1.0, MASK_BIG).astype(BF16)], axis=1)
        return jnp.dot(lhs, qs_ref[...], preferred_element_type=F32)

    near0 = jnp.maximum(row0 - tq, 0)
    n_far = (near0 + ATT_TK - 1) // ATT_TK
    fkid0 = lax.broadcasted_iota(jnp.int32, (ATT_TK, tq), 0)
    n0 = pl.multiple_of(near0, tq)
    first = (i == 0).astype(jnp.int32)
    nkid = n0 + lax.broadcasted_iota(jnp.int32, (NEAR_W, tq), 0)
    nqid = row0 + lax.broadcasted_iota(jnp.int32, (NEAR_W, tq), 1)

    def far_tile(j):
        r0 = pl.multiple_of(j * ATT_TK, ATT_TK)
        keep = (sc_ref[pl.ds(r0, ATT_TK), :] >= thr) & (fkid0 + r0 < near0)
        return ckv_ref[pl.ds(r0, ATT_TK), :], ckvt_ref[:, pl.ds(r0, ATT_TK)], keep

    def near_tile():
        keep = (sc_ref[pl.ds(n0, NEAR_W), :] >= thr) & (nkid <= nqid)
        return ckv_ref[pl.ds(n0, NEAR_W), :], ckvt_ref[:, pl.ds(n0, NEAR_W)], keep

    set_shift_rows(jnp.full((1, H * tq), -1.0, F32))
    ckv_n, ckvt_n, keep_n = near_tile()
    lg = qk(ckv_n, keep_n) + tbl_ref[first]
    m_n = jnp.maximum(jnp.max(lg, axis=0, keepdims=True), M_INIT)
    p = jnp.exp2(lg - m_n)
    l_n = jnp.sum(p, axis=0, keepdims=True)
    accn_ref[...] = jnp.dot(ckvt_n, p.astype(BF16), preferred_element_type=F32)
    s_n = m_n + 1.0

    s_f = jnp.clip(s_n, 1.0, SHIFT_MAX).astype(BF16).astype(F32)
    set_shift_rows(-s_f)
    acc_ref[...] = jnp.zeros(acc_ref.shape, F32)

    def far_body(j, l_f):
        ckv_t, ckvt_t, keep = far_tile(j)
        p = jnp.exp2(qk(ckv_t, keep))
        acc_ref[...] += jnp.dot(ckvt_t, p.astype(BF16), preferred_element_type=F32)
        return l_f + jnp.sum(p, axis=0, keepdims=True)

    l_f = lax.fori_loop(0, n_far, far_body, jnp.zeros((1, H * tq), F32))

    has_far = n_far > 0
    s_max = jnp.maximum(s_n, s_f)
    w_n = jnp.where(has_far, jnp.exp2(s_n - s_max), 1.0)
    w_f = jnp.where(has_far, jnp.exp2(s_f - s_max), 0.0)
    l_tot = l_n * w_n + l_f * w_f
    acc_ref[...] = (accn_ref[...] * w_n + acc_ref[...] * w_f) * (1.0 / l_tot)
    in_range = (l_tot > L_TOT_MIN) & (l_tot < L_TOT_MAX)
    n_bad = jnp.sum(jnp.where(in_range, 0.0, 1.0))

    @pl.when(n_bad > 0.0)
    def _():
        set_shift_rows(jnp.full((1, H * tq), -1.0, F32))
        m_ref[...] = jnp.full(m_ref.shape, M_INIT, F32)
        l_ref[...] = jnp.zeros(l_ref.shape, F32)
        acc_ref[...] = jnp.zeros(acc_ref.shape, F32)

        def attend(ckv_t, ckvt_t, keep, bias):
            lg = qk(ckv_t, keep)
            if bias is not None:
                lg = lg + bias
            m_prev = m_ref[...]
            m_new = jnp.maximum(m_prev, jnp.max(lg, axis=0, keepdims=True))
            alpha = jnp.exp2(m_prev - m_new)
            p = jnp.exp2(lg - m_new)
            l_ref[...] = alpha * l_ref[...] + jnp.sum(p, axis=0, keepdims=True)
            m_ref[...] = m_new
            pv = jnp.dot(ckvt_t, p.astype(BF16), preferred_element_type=F32)
            acc_ref[...] = alpha * acc_ref[...] + pv

        def slow_body(j, carry):
            attend(*far_tile(j), None)
            return carry

        lax.fori_loop(0, n_far, slow_body, 0)
        attend(*near_tile(), tbl_ref[first])
        acc_ref[...] = acc_ref[...] * (1.0 / l_ref[...])

    ot = acc_ref[...].astype(BF16)
    o_st = jnp.concatenate([ot[:, h * tq:(h + 1) * tq] for h in range(H)], axis=0)
    yt = jnp.dot(wuvt_ref[...], o_st, preferred_element_type=F32)
    o_ref[...] = yt.T.astype(BF16)


def _dsa(qlt, iqt, iwt, ik, ckv, ckvt, tbl, wuvt, layer, B, S):
    tq = Q_BLOCK
    nb = S // tq
    RL = A_HEADS * KV_RANK
    QI = IDX_HEADS * IDX_DIM
    k_sel = min(TOPK_MAX, S // 4)
    assert IDX_TK % k_sel == 0 and k_sel % SUBLANES == 0
    return pl.pallas_call(
        functools.partial(_dsa_kernel, k_sel=k_sel, seq=S),
        out_shape=jax.ShapeDtypeStruct((B * S, A_WIDTH), BF16),
        grid=(B, nb),
        in_specs=[
            pl.BlockSpec((None, RL, tq), lambda b, i: (b, 0, i)),
            pl.BlockSpec((None, QI, tq), lambda b, i: (b, 0, i)),
            pl.BlockSpec((None, IDX_HEADS, tq), lambda b, i: (b, 0, i)),
            pl.BlockSpec((None, S, IDX_DIM), lambda b, i: (b, 0, 0)),
            pl.BlockSpec((None, S, KV_RANK), lambda b, i: (b, 0, 0)),
            pl.BlockSpec((None, KV_RANK, S), lambda b, i: (b, 0, 0)),
            pl.BlockSpec((2, NEAR_W, A_HEADS * tq), lambda b, i: (0, 0, 0)),
            pl.BlockSpec((None, A_WIDTH, RL), lambda b, i: (layer, 0, 0)),
        ],
        out_specs=pl.BlockSpec((tq, A_WIDTH), lambda b, i: (b * nb + i, 0)),
        scratch_shapes=[
            pltpu.VMEM((S, tq), F32),
            pltpu.VMEM((S, tq), F32),
            pltpu.VMEM((SORT_N * SUBLANES, tq), F32),
            pltpu.VMEM((1, tq), jnp.int32),
            pltpu.VMEM((KV_RANK + tq, A_HEADS * tq), BF16),
            pltpu.VMEM((IDX_DIM, IDX_HEADS * tq), BF16),
            pltpu.VMEM((1, A_HEADS * tq), F32),
            pltpu.VMEM((1, A_HEADS * tq), F32),
            pltpu.VMEM((KV_RANK, A_HEADS * tq), F32),
            pltpu.VMEM((KV_RANK, A_HEADS * tq), F32),
            pltpu.VMEM((k_sel, tq), F32),
        ],
        compiler_params=_params("parallel", "arbitrary"),
        name="dsa",
    )(qlt, iqt, iwt, ik, ckv, ckvt, tbl, wuvt)


def _memkv_kernel(mem_ref, g_ref, w_ref, kbd_ref, vbd_ref):
    M = mem_ref.shape[0]
    mn = _rms(mem_ref[...], g_ref[...]).astype(BF16)
    kv = jnp.dot(mn, w_ref[...], preferred_element_type=F32)
    kt = kv[:, :C_WIDTH].T
    v = kv[:, C_WIDTH:]
    r = lax.broadcasted_iota(jnp.int32, (C_WIDTH, C_HEADS * M), 0) // C_HEAD_DIM
    c = lax.broadcasted_iota(jnp.int32, (C_WIDTH, C_HEADS * M), 1) // M
    kbd_ref[...] = jnp.where(r == c, jnp.concatenate([kt] * C_HEADS, axis=1), 0.0).astype(BF16)
    r = lax.broadcasted_iota(jnp.int32, (C_HEADS * M, C_WIDTH), 0) // M
    c = lax.broadcasted_iota(jnp.int32, (C_HEADS * M, C_WIDTH), 1) // C_HEAD_DIM
    vbd_ref[...] = jnp.where(r == c, jnp.concatenate([v] * C_HEADS, axis=0), 0.0).astype(BF16)


def _memkv(mem, g, w, layer):
    B, M, D = mem.shape
    return pl.pallas_call(
        _memkv_kernel,
        out_shape=[jax.ShapeDtypeStruct((B, C_WIDTH, C_HEADS * M), BF16),
                   jax.ShapeDtypeStruct((B, C_HEADS * M, C_WIDTH), BF16)],
        grid=(B,),
        in_specs=[pl.BlockSpec((None, M, D), lambda b: (b, 0, 0)),
                  pl.BlockSpec((None, 1, D), lambda b: (layer, 0, 0)),
                  pl.BlockSpec((None, D, 2 * C_WIDTH), lambda b: (layer, 0, 0))],
        out_specs=[pl.BlockSpec((None, C_WIDTH, C_HEADS * M), lambda b: (b, 0, 0)),
                   pl.BlockSpec((None, C_HEADS * M, C_WIDTH), lambda b: (b, 0, 0))],
        compiler_params=_params("parallel"),
        name="memkv",
    )(mem, g, w)


def _mix_kernel(h_ref, g_ref, wg_ref, ya_ref, z_ref, zh_ref, qc_ref, kbd_ref, vbd_ref, wpool_ref, ps_ref,
                wa_ref, wb_ref, wc_ref, wo_ref, o_ref, zx_ref):
    tm, D = h_ref.shape
    i = pl.program_id(1)
    M = kbd_ref.shape[1] // C_HEADS
    h = h_ref[...]
    u = _rms(h, g_ref[...]).astype(BF16)

    z = z_ref[...]
    zx_ref[0:POOL_HALO, :] = jnp.where(i == 0, 0.0, zh_ref[...])
    zx_ref[POOL_HALO:, :] = z
    pos = i * tm + lax.broadcasted_iota(jnp.int32, (tm, 1), 0) + 1
    grp = lax.broadcasted_iota(jnp.int32, (1, POOL_WIDTH), 1) // POOL_GROUP
    win = z
    pooled = jnp.zeros_like(z)
    k = 1
    for gi, w in enumerate(POOL_WINDOWS):
        while k < w:
            win = win + zx_ref[POOL_HALO - k:POOL_HALO - k + tm, :]
            k += 1
        cnt = jnp.minimum(pos, w).astype(F32)
        pooled = jnp.where(grp == gi, win / cnt - z, pooled)
    yb = jnp.dot(pooled.astype(BF16), wpool_ref[...], preferred_element_type=F32) * ps_ref[...]

    lg = jnp.dot(qc_ref[...], kbd_ref[...], preferred_element_type=F32)
    ps = []
    for hc in range(C_HEADS):
        seg = lg[:, hc * M:(hc + 1) * M]
        e = jnp.exp(seg - jnp.max(seg, axis=-1, keepdims=True))
        ps.append(e / jnp.sum(e, axis=-1, keepdims=True))
    p = jnp.concatenate(ps, axis=1).astype(BF16)
    yc = jnp.dot(p, vbd_ref[...], preferred_element_type=F32)

    def gate(n):
        pre = jnp.dot(u, wg_ref[:, n * D:(n + 1) * D], preferred_element_type=F32)
        return jax.nn.sigmoid(pre)

    merged = gate(0) * jnp.dot(ya_ref[...], wa_ref[...], preferred_element_type=F32)
    merged += gate(1) * jnp.dot(yb.astype(BF16), wb_ref[...], preferred_element_type=F32)
    merged += gate(2) * jnp.dot(yc.astype(BF16), wc_ref[...], preferred_element_type=F32)
    o_ref[...] = h + jnp.dot(merged.astype(BF16), wo_ref[...], preferred_element_type=F32)


def _mix(h, g, ya, z, qc, kbd, vbd, wp, layer, B, S):
    T, D = h.shape
    tm = min(MIX_TM, S)
    nt = S // tm
    hb = tm // POOL_HALO

    def wspec(arr):
        return pl.BlockSpec((None,) + arr.shape[1:], lambda b, i: (layer,) + (0,) * (arr.ndim - 1))

    tok = lambda w: pl.BlockSpec((tm, w), lambda b, i: (b * nt + i, 0))
    names = ("wpool", "ps", "wa", "wb", "wc", "wo")
    return pl.pallas_call(
        _mix_kernel,
        out_shape=jax.ShapeDtypeStruct((T, D), F32),
        grid=(B, nt),
        in_specs=[tok(D), wspec(g), wspec(wp["wg"]), tok(A_WIDTH), tok(POOL_WIDTH),
                  pl.BlockSpec((POOL_HALO, POOL_WIDTH), lambda b, i: (jnp.maximum((b * nt + i) * hb - 1, 0), 0)),
                  tok(C_WIDTH),
                  pl.BlockSpec((None,) + kbd.shape[1:], lambda b, i: (b, 0, 0)),
                  pl.BlockSpec((None,) + vbd.shape[1:], lambda b, i: (b, 0, 0))]
                 + [wspec(wp[k]) for k in names],
        out_specs=tok(D),
        scratch_shapes=[pltpu.VMEM((tm + POOL_HALO, POOL_WIDTH), F32)],
        compiler_params=_params("parallel", "parallel"),
        name="mix",
    )(h, g, wp["wg"], ya, z, z, qc, kbd, vbd, *[wp[k] for k in names])


def _block_diag(blocks):
    L, G, r, c = blocks.shape
    eye = jnp.eye(G, dtype=blocks.dtype)
    return jnp.einsum("lgrc,gh->lgrhc", blocks, eye).reshape(L, G * r, G * c)


def _prep_weights(w_in, kv_norm, w_uk, w_uv, w_pool, pool_scale, w_branch_a, w_branch_b, w_branch_c, w_out):
    L, D, _ = w_in.shape
    sizes = (A_WIDTH, KV_RANK, IDX_HEADS * IDX_DIM, IDX_DIM, IDX_HEADS, POOL_WIDTH, C_WIDTH, N_BRANCH * D)
    offs = np.concatenate([[0], np.cumsum(sizes)])
    wq, wckv, wiq, wik, wiw, wz, wqc, wg = [w_in[:, :, offs[n]:offs[n + 1]] for n in range(len(sizes))]
    proj = dict(
        wq=wq.astype(BF16),
        wukt=_block_diag(jnp.transpose(w_uk, (0, 2, 1, 3))).astype(BF16),
        wckv=wckv.astype(BF16),
        kvg=kv_norm[:, None, :],
        wiqt=jnp.transpose(wiq, (0, 2, 1)).astype(BF16),
        wik=jnp.pad(wik, ((0, 0), (0, 0), (0, LANES - IDX_DIM))).astype(BF16),
        wiwt=jnp.pad(jnp.transpose(wiw, (0, 2, 1)), ((0, 0), (0, BF16_ROWS - IDX_HEADS), (0, 0))).astype(BF16),
        wz=wz.astype(BF16),
        wqc=wqc.astype(BF16),
    )
    wuvt = _block_diag(jnp.transpose(w_uv, (0, 2, 3, 1))).astype(BF16)
    mix = dict(
        wg=wg.astype(BF16),
        wpool=_block_diag(w_pool).astype(BF16),
        ps=pool_scale[:, None, :],
        wa=w_branch_a.astype(BF16), wb=w_branch_b.astype(BF16), wc=w_branch_c.astype(BF16),
        wo=w_out.astype(BF16),
    )
    return proj, wuvt, mix


def kernel(x, mem, rel_bias, ffn1_norm, ffn1_w_in, ffn1_w_out, mix_norm, w_in, kv_norm, w_uk, w_uv, w_pool,
           pool_scale, mem_norm, w_mem_kv, w_branch_a, w_branch_b, w_branch_c, w_out, ffn2_norm, ffn2_w_in,
           ffn2_w_out, final_norm):
    B, S, D = x.shape
    L = w_in.shape[0]
    assert S % Q_BLOCK == 0 and S % IDX_TK == 0 and S % ATT_TK == 0 and S >= NEAR_W

    proj_w, wuvt, mix_w = _prep_weights(w_in, kv_norm, w_uk, w_uv, w_pool, pool_scale,
                                        w_branch_a, w_branch_b, w_branch_c, w_out)
    f1_in, f1_out = ffn1_w_in.astype(BF16), ffn1_w_out.astype(BF16)
    f2_in, f2_out = ffn2_w_in.astype(BF16), ffn2_w_out.astype(BF16)
    f1_g, f2_g, mx_g, mm_g = (a[:, None, :] for a in (ffn1_norm, ffn2_norm, mix_norm, mem_norm))
    fg = final_norm[None, :]
    wmem = w_mem_kv.astype(BF16)
    tbl = _bias_tables(rel_bias)

    h = x.reshape(B * S, D)
    for l in range(L):
        h = _ffn(h, f1_g, f1_in, f1_out, fg, l, final=False)
        qlt, ckv, ckvt, iqt, ik, iwt, z, qc = _proj(h, mx_g, proj_w, l, B, S)
        ya = _dsa(qlt, iqt, iwt, ik.reshape(B, S, IDX_DIM), ckv.reshape(B, S, KV_RANK), ckvt, tbl, wuvt, l, B, S)
        kbd, vbd = _memkv(mem, mm_g, wmem, l)
        h = _mix(h, mx_g, ya, z, qc, kbd, vbd, mix_w, l, B, S)
        h = _ffn(h, f2_g, f2_in, f2_out, fg, l, final=(l == L - 1))
    return h.reshape(B, S, D)
```

```python
import functools
import math

import numpy as np
import jax
import jax.numpy as jnp
from jax import lax
from jax.experimental import pallas as pl
from jax.experimental.pallas import tpu as pltpu

F32 = jnp.float32
BF16 = jnp.bfloat16

A_HEADS = 8
A_HEAD_DIM = 64
A_WIDTH = A_HEADS * A_HEAD_DIM
KV_RANK = 128
IDX_HEADS = 8
IDX_DIM = 32
TOPK_MAX = 256
Q_BLOCK = 128
POOL_WINDOWS = (2, 4, 8, 16)
POOL_GROUP = 64
POOL_WIDTH = len(POOL_WINDOWS) * POOL_GROUP
POOL_HALO = max(POOL_WINDOWS)
C_HEADS = 4
C_HEAD_DIM = 64
C_WIDTH = C_HEADS * C_HEAD_DIM
N_BRANCH = 3
REL_BUCKETS = 32
REL_MAX_DIST = 128
EPS = 1e-6
NEG = -1e30
M_INIT = -1e29
LOG2E = math.log2(math.e)
MASK_BIG = -NEG
SHIFT_MAX = 2.0 ** 20
L_TOT_MIN = 2.0 ** -60
L_TOT_MAX = 2.0 ** 100

LANES = 128
SUBLANES = 8
BF16_ROWS = 16
VMEM_LIMIT = 52 * 1024 * 1024
FFN_TM = 512
PROJ_TM = 512
MIX_TM = 512
IDX_TK = 512
ATT_TK = 512
NEAR_W = 2 * Q_BLOCK
COUNT_PARTIALS = 8
SORT_N = 16
LAYER_ROWS = IDX_TK // SORT_N
LAYER_FULL_STEPS = 5
VALUE_MID_STEPS = 12
BISECT_FIXED_STEPS = 5
KEY_MIN_NORMAL = 0x00800000
F32_BIG = 3.0e38
BISECT_STEPS_PER_TEST = 4


def _oddeven_merge_sort(n):
    def merge(lo, hi, r):
        step = r * 2
        if step < hi - lo:
            yield from merge(lo, hi, step)
            yield from merge(lo + r, hi, step)
            yield from ((i, i + r) for i in range(lo + r, hi - r, step))
        else:
            yield (lo, lo + r)

    def sort(lo, hi):
        if hi - lo >= 1:
            mid = lo + (hi - lo) // 2
            yield from sort(lo, mid)
            yield from sort(mid + 1, hi)
            yield from merge(lo, hi, 1)

    return tuple(sort(0, n - 1))


_SORT_NETWORK = _oddeven_merge_sort(SORT_N)

INT_MIN = -2 ** 31


def _rms(x, g):
    return x * lax.rsqrt(jnp.mean(x * x, axis=-1, keepdims=True) + EPS) * g


def _params(*sem):
    return pltpu.CompilerParams(dimension_semantics=sem, vmem_limit_bytes=VMEM_LIMIT)


def _dot_nt(a, b):
    return lax.dot_general(a, b, (((1,), (1,)), ((), ())), preferred_element_type=F32)


def _ffn_kernel(x_ref, g_ref, wa_ref, wb_ref, wo_ref, fg_ref, o_ref, xn_ref, acc_ref, *, n_ff, final):
    j = pl.program_id(1)

    @pl.when(j == 0)
    def _():
        xn_ref[...] = _rms(x_ref[...], g_ref[...]).astype(BF16)
        acc_ref[...] = jnp.zeros_like(acc_ref)

    xn = xn_ref[...]
    a = jnp.dot(xn, wa_ref[...], preferred_element_type=F32)
    b = jnp.dot(xn, wb_ref[...], preferred_element_type=F32)
    act = (a * jax.nn.sigmoid(a) * b).astype(BF16)
    acc_ref[...] += jnp.dot(act, wo_ref[...], preferred_element_type=F32)

    @pl.when(j == n_ff - 1)
    def _():
        y = x_ref[...] + 0.5 * acc_ref[...]
        if final:
            y = _rms(y, fg_ref[...])
        o_ref[...] = y


def _ffn(h, g, w_in, w_out, fg, layer, *, final):
    T, D = h.shape
    F = w_out.shape[1]
    n_ff = 2 if F % (2 * LANES) == 0 else 1
    fc = F // n_ff
    tm = min(FFN_TM, T)
    return pl.pallas_call(
        functools.partial(_ffn_kernel, n_ff=n_ff, final=final),
        out_shape=jax.ShapeDtypeStruct((T, D), F32),
        grid=(T // tm, n_ff),
        in_specs=[
            pl.BlockSpec((tm, D), lambda i, j: (i, 0)),
            pl.BlockSpec((None, 1, D), lambda i, j: (layer, 0, 0)),
            pl.BlockSpec((None, D, fc), lambda i, j: (layer, 0, j)),
            pl.BlockSpec((None, D, fc), lambda i, j: (layer, 0, n_ff + j)),
            pl.BlockSpec((None, fc, D), lambda i, j: (layer, j, 0)),
            pl.BlockSpec((1, D), lambda i, j: (0, 0)),
        ],
        out_specs=pl.BlockSpec((tm, D), lambda i, j: (i, 0)),
        scratch_shapes=[pltpu.VMEM((tm, D), BF16), pltpu.VMEM((tm, D), F32)],
        compiler_params=_params("parallel", "arbitrary"),
        name="ffn",
    )(h, g, w_in, w_in, w_out, fg)


_PROJ_W = ("wq", "wukt", "wckv", "kvg", "wiqt", "wik", "wiwt", "wz", "wqc")


def _proj_kernel(h_ref, g_ref, wq_ref, wukt_ref, wckv_ref, kvg_ref, wiqt_ref, wik_ref, wiwt_ref, wz_ref, wqc_ref,
                 qlt_ref, ckv_ref, ckvt_ref, iqt_ref, ik_ref, iwt_ref, z_ref, qc_ref):
    u = _rms(h_ref[...], g_ref[...]).astype(BF16)
    q = jnp.dot(u, wq_ref[...], preferred_element_type=F32).astype(BF16)
    qlt_ref[...] = (_dot_nt(wukt_ref[...], q) * (A_HEAD_DIM ** -0.5 * LOG2E)).astype(BF16)
    c = _rms(jnp.dot(u, wckv_ref[...], preferred_element_type=F32), kvg_ref[...])
    ckv_ref[...] = c.astype(BF16)
    ckvt_ref[...] = c.T.astype(BF16)
    iqt_ref[...] = _dot_nt(wiqt_ref[...], u).astype(BF16)
    ik_ref[...] = jnp.dot(u, wik_ref[...], preferred_element_type=F32)[:, :IDX_DIM].astype(BF16)
    iwt = _dot_nt(wiwt_ref[...], u)[:IDX_HEADS]
    iwt_ref[...] = iwt * ((IDX_DIM ** -0.5) * (IDX_HEADS ** -0.5))
    z_ref[...] = jnp.dot(u, wz_ref[...], preferred_element_type=F32)
    qc = jnp.dot(u, wqc_ref[...], preferred_element_type=F32) * (C_HEAD_DIM ** -0.5)
    qc_ref[...] = qc.astype(BF16)


def _proj(h, g, wp, layer, B, S):
    T, D = h.shape
    tm = min(PROJ_TM, S)
    nt = S // tm
    RL = A_HEADS * KV_RANK
    QI = IDX_HEADS * IDX_DIM

    def wspec(arr):
        return pl.BlockSpec((None,) + arr.shape[1:], lambda b, i: (layer,) + (0,) * (arr.ndim - 1))

    tok = lambda w: pl.BlockSpec((tm, w), lambda b, i: (b * nt + i, 0))
    tokt = lambda w: pl.BlockSpec((None, w, tm), lambda b, i: (b, 0, i))
    return pl.pallas_call(
        _proj_kernel,
        out_shape=[
            jax.ShapeDtypeStruct((B, RL, S), BF16),
            jax.ShapeDtypeStruct((T, KV_RANK), BF16),
            jax.ShapeDtypeStruct((B, KV_RANK, S), BF16),
            jax.ShapeDtypeStruct((B, QI, S), BF16),
            jax.ShapeDtypeStruct((T, IDX_DIM), BF16),
            jax.ShapeDtypeStruct((B, IDX_HEADS, S), F32),
            jax.ShapeDtypeStruct((T, POOL_WIDTH), F32),
            jax.ShapeDtypeStruct((T, C_WIDTH), BF16),
        ],
        grid=(B, nt),
        in_specs=[tok(D), wspec(g)] + [wspec(wp[k]) for k in _PROJ_W],
        out_specs=[tokt(RL), tok(KV_RANK), tokt(KV_RANK), tokt(QI), tok(IDX_DIM), tokt(IDX_HEADS),
                   tok(POOL_WIDTH), tok(C_WIDTH)],
        compiler_params=_params("parallel", "parallel"),
        name="proj",
    )(h, g, *[wp[k] for k in _PROJ_W])


def _t5_bucket_np(dist):
    max_exact = REL_BUCKETS // 2
    n = np.maximum(dist, 0)
    nf = np.maximum(n, 1).astype(np.float32)
    large = max_exact + (np.log(nf / np.float32(max_exact)) / np.float32(math.log(REL_MAX_DIST / max_exact))
                         * np.float32(REL_BUCKETS - max_exact)).astype(np.int32)
    large = np.minimum(large, REL_BUCKETS - 1)
    return np.where(n < max_exact, n, large).astype(np.int32)


def _bias_kernel(rb_ref, bucket_ref, o_ref):
    for k in range(2):
        bk = bucket_ref[k]
        for h in range(A_HEADS):
            far = rb_ref[REL_BUCKETS - 1, h]
            acc = jnp.zeros(bk.shape, F32)
            for b in range(REL_BUCKETS - 1):
                acc = jnp.where(bk == b, (rb_ref[b, h] - far) * LOG2E, acc)
            o_ref[k, :, h * Q_BLOCK:(h + 1) * Q_BLOCK] = acc


def _bias_tables(rel_bias):
    c = np.arange(NEAR_W)[:, None]
    r = np.arange(Q_BLOCK)[None, :]
    buckets = np.stack([_t5_bucket_np(r + Q_BLOCK - c), _t5_bucket_np(r - c)])
    return pl.pallas_call(
        _bias_kernel,
        out_shape=jax.ShapeDtypeStruct((2, NEAR_W, A_HEADS * Q_BLOCK), F32),
        in_specs=[pl.BlockSpec(memory_space=pltpu.SMEM), pl.BlockSpec(memory_space=pltpu.VMEM)],
        out_specs=pl.BlockSpec(memory_space=pltpu.VMEM),
        name="bias_tables",
    )(rel_bias, jnp.asarray(buckets))


def _key_to_f32(k):
    return lax.bitcast_convert_type(jnp.where(k < 0, INT_MIN - k, k), F32)


def _f32_to_key(x):
    b = lax.bitcast_convert_type(x, jnp.int32)
    return jnp.where(b < 0, INT_MIN - b, b)


def _dsa_kernel(qlt_ref, iqt_ref, iwt_ref, ik_ref, ckv_ref, ckvt_ref, tbl_ref, wuvt_ref, o_ref,
                sc_ref, lay_ref, lmax_ref, thr_ref, qs_ref, iqs_ref, m_ref, l_ref, acc_ref, accn_ref, fold_ref,
                *, k_sel, seq):
    tq = Q_BLOCK
    H = A_HEADS
    i = pl.program_id(1)
    row0 = i * tq
    n_idx = (row0 + tq + IDX_TK - 1) // IDX_TK

    eye_b = lax.broadcasted_iota(jnp.int32, (tq, tq), 0) == lax.broadcasted_iota(jnp.int32, (tq, tq), 1)
    for h in range(H):
        qs_ref[0:KV_RANK, h * tq:(h + 1) * tq] = qlt_ref[h * KV_RANK:(h + 1) * KV_RANK, :]
        iqs_ref[:, h * tq:(h + 1) * tq] = iqt_ref[h * IDX_DIM:(h + 1) * IDX_DIM, :]

    w_rows = [iwt_ref[h:h + 1, :] for h in range(IDX_HEADS)]
    qid = row0 + lax.broadcasted_iota(jnp.int32, (IDX_TK, tq), 1)
    kid0 = lax.broadcasted_iota(jnp.int32, (IDX_TK, tq), 0)
    fold_ref[...] = jnp.full(fold_ref.shape, -jnp.inf, F32)
    lmax_ref[...] = jnp.full(lmax_ref.shape, -jnp.inf, F32)

    def idx_body(j, carry):
        r0 = pl.multiple_of(j * IDX_TK, IDX_TK)
        d = jnp.dot(ik_ref[pl.ds(r0, IDX_TK), :], iqs_ref[...], preferred_element_type=F32)
        s = jnp.maximum(d[:, 0:tq], 0.0) * w_rows[0]
        for h in range(1, IDX_HEADS):
            s = s + jnp.maximum(d[:, h * tq:(h + 1) * tq], 0.0) * w_rows[h]
        s = jnp.where(kid0 + r0 <= qid, s, -jnp.inf)
        sc_ref[pl.ds(r0, IDX_TK), :] = s
        f = s[0:k_sel]
        for q in range(1, IDX_TK // k_sel):
            f = jnp.maximum(f, s[q * k_sel:(q + 1) * k_sel])
        fold_ref[...] = jnp.maximum(fold_ref[...], f)
        n_grp = IDX_TK // (SORT_N * SUBLANES)
        x = [jnp.concatenate([s[(g * SORT_N + r) * SUBLANES:(g * SORT_N + r + 1) * SUBLANES]
                              for g in range(n_grp)], axis=0) for r in range(SORT_N)]
        for a, b in _SORT_NETWORK:
            x[a], x[b] = jnp.maximum(x[a], x[b]), jnp.minimum(x[a], x[b])
        for r in range(SORT_N):
            lay_ref[pl.ds(r0 + r * LAYER_ROWS, LAYER_ROWS), :] = x[r]
            m = x[r][0:SUBLANES]
            for g in range(1, n_grp):
                m = jnp.maximum(m, x[r][g * SUBLANES:(g + 1) * SUBLANES])
            lmax_ref[r * SUBLANES:(r + 1) * SUBLANES, :] = jnp.maximum(lmax_ref[r * SUBLANES:(r + 1) * SUBLANES, :], m)
        return carry

    lax.fori_loop(0, n_idx, idx_body, 0)

    n_par = COUNT_PARTIALS

    def count(pred):
        def body(j, acc):
            r0 = pl.multiple_of(j * IDX_TK, IDX_TK)
            ind = jnp.where(pred(sc_ref[pl.ds(r0, IDX_TK), :], r0), 1.0, 0.0)
            return acc + jnp.sum(ind.reshape(n_par, IDX_TK // n_par, tq), axis=0)
        acc = lax.fori_loop(0, n_idx, body, jnp.zeros((IDX_TK // n_par, tq), F32))
        return jnp.sum(acc, axis=0, keepdims=True)

    kf = float(k_sel)
    fold = fold_ref[...]
    lo0 = _f32_to_key(jnp.min(fold, axis=0, keepdims=True))
    hi0 = _f32_to_key(jnp.max(fold, axis=0, keepdims=True))

    def n_open(lo, hi):
        return jnp.sum((lo < hi).astype(F32))

    def run_bisection(count_fn, lo, hi):
        for value_mid, steps in ((True, VALUE_MID_STEPS - LAYER_FULL_STEPS), (False, BISECT_FIXED_STEPS)):
            step = functools.partial(bis_step, count_fn, value_mid)
            lo, hi = lax.fori_loop(0, steps, lambda _, st: step(*st), (lo, hi))

        def body(st):
            lo_n, hi_n = lax.fori_loop(0, BISECT_STEPS_PER_TEST, lambda _, s2: step(*s2), (st[0], st[1]))
            return lo_n, hi_n, n_open(lo_n, hi_n)

        lo, hi, _ = lax.while_loop(lambda st: st[2] > 0.0, body, (lo, hi, n_open(lo, hi)))
        return lo

    def bis_step(count_fn, value_mid, lo, hi):
        d = hi - lo
        mid = lo + lax.shift_right_logical(d, 1) + (d & 1)
        if value_mid:
            lo_f = jnp.maximum(_key_to_f32(lo), -F32_BIG)
            hi_f = jnp.minimum(_key_to_f32(hi), F32_BIG)
            wide = (hi_f - lo_f) > 0.5 * jnp.maximum(jnp.abs(lo_f), jnp.abs(hi_f))
            vmid = jnp.minimum(jnp.maximum(_f32_to_key(0.5 * lo_f + 0.5 * hi_f), lo + 1), hi)
            mid = jnp.where(wide, vmid, mid)
        at_zero = (lo == 0) & (hi >= KEY_MIN_NORMAL)
        mid = jnp.where((lo < 0) & (hi >= 0), 0, jnp.where(at_zero, KEY_MIN_NORMAL, mid))
        thr = _key_to_f32(mid)
        cnt = count_fn(lambda s, r0: s >= thr)
        active = lo < hi
        ge = cnt >= kf
        lo_n = jnp.where(active & ge, mid, lo)
        hi_lt = jnp.where(at_zero, 0, mid - 1)
        hi_n = jnp.where(active, jnp.where(ge, jnp.where(cnt == kf, mid, hi), hi_lt), hi)
        return lo_n, hi_n

    lo, hi = lax.fori_loop(0, LAYER_FULL_STEPS, lambda _, st: bis_step(count, True, *st), (lo0, hi0))
    layer_max = [jnp.max(lmax_ref[r * SUBLANES:(r + 1) * SUBLANES, :], axis=0, keepdims=True)
                 for r in range(SORT_N)]
    lo_f = _key_to_f32(lo)
    n_alive = sum(jnp.max(jnp.where(layer_max[r] >= lo_f, 1.0, 0.0)) for r in range(SORT_N))
    n_quarters = (n_alive.astype(jnp.int32) + SORT_N // 4 - 1) // (SORT_N // 4)

    for k in range(1, 5):
        rows = k * (SORT_N // 4) * LAYER_ROWS

        def count_top(pred, rows=rows):
            def body(j, acc):
                r0 = pl.multiple_of(j * IDX_TK, IDX_TK)
                ind = jnp.where(pred(lay_ref[pl.ds(r0, rows), :], r0), 1.0, 0.0)
                return acc + jnp.sum(ind.reshape(n_par, rows // n_par, tq), axis=0)
            acc = lax.fori_loop(0, n_idx, body, jnp.zeros((rows // n_par, tq), F32))
            return jnp.sum(acc, axis=0, keepdims=True)

        cond = (n_quarters <= 1) if k == 1 else ((n_quarters >= 4) if k == 4 else (n_quarters == k))

        @pl.when(cond)
        def _(count_top=count_top):
            thr_ref[...] = run_bisection(count_top, lo, hi)

    lo = thr_ref[...]
    thr = _key_to_f32(lo)

    cnt_ge = count(lambda s, r0: s >= thr)
    n_tie = jnp.sum((cnt_ge > kf).astype(F32))

    @pl.when(n_tie > 0.0)
    def _():
        need = kf - count(lambda s, r0: s > thr)

        def pos_body(_, st):
            plo, phi = st
            mid = (plo + phi) // 2
            c = count(lambda s, r0: (s == thr) & (kid0 + r0 <= mid))
            ok = c >= need
            return jnp.where(ok, plo, mid + 1), jnp.where(ok, mid, phi)

        plo0 = jnp.zeros((1, tq), jnp.int32)
        phi0 = jnp.full((1, tq), seq - 1, jnp.int32)
        last, _ = lax.fori_loop(0, max(1, (seq - 1).bit_length()), pos_body, (plo0, phi0))

        def drop_body(j, carry):
            r0 = pl.multiple_of(j * IDX_TK, IDX_TK)
            s = sc_ref[pl.ds(r0, IDX_TK), :]
            sc_ref[pl.ds(r0, IDX_TK), :] = jnp.where((s == thr) & (kid0 + r0 > last), -jnp.inf, s)
            return carry

        lax.fori_loop(0, n_idx, drop_body, 0)

    def set_shift_rows(neg_shift):
        for h in range(H):
            blk = jnp.where(eye_b, neg_shift[:, h * tq:(h + 1) * tq], 0.0)
            qs_ref[KV_RANK:, h * tq:(h + 1) * tq] = blk.astype(BF16)

    def qk(ckv_t, keep):
        lhs = jnp.concatenate([ckv_t, jnp.where(keep, 1.0, MASK_BIG).astype(BF16)], axis=1)
        return jnp.dot(lhs, qs_ref[...], preferred_element_type=F32)

    near0 = jnp.maximum(row0 - tq, 0)
    n_far = (near0 + ATT_TK - 1) // ATT_TK
    fkid0 = lax.broadcasted_iota(jnp.int32, (ATT_TK, tq), 0)
    n0 = pl.multiple_of(near0, tq)
    first = (i == 0).astype(jnp.int32)
    nkid = n0 + lax.broadcasted_iota(jnp.int32, (NEAR_W, tq), 0)
    nqid = row0 + lax.broadcasted_iota(jnp.int32, (NEAR_W, tq), 1)

    def far_tile(j):
        r0 = pl.multiple_of(j * ATT_TK, ATT_TK)
        keep = (sc_ref[pl.ds(r0, ATT_TK), :] >= thr) & (fkid0 + r0 < near0)
        return ckv_ref[pl.ds(r0, ATT_TK), :], ckvt_ref[:, pl.ds(r0, ATT_TK)], keep

    def near_tile():
        keep = (sc_ref[pl.ds(n0, NEAR_W), :] >= thr) & (nkid <= nqid)
        return ckv_ref[pl.ds(n0, NEAR_W), :], ckvt_ref[:, pl.ds(n0, NEAR_W)], keep

    set_shift_rows(jnp.full((1, H * tq), -1.0, F32))
    ckv_n, ckvt_n, keep_n = near_tile()
    lg = qk(ckv_n, keep_n) + tbl_ref[first]
    m_n = jnp.maximum(jnp.max(lg, axis=0, keepdims=True), M_INIT)
    p = jnp.exp2(lg - m_n)
    l_n = jnp.sum(p, axis=0, keepdims=True)
    accn_ref[...] = jnp.dot(ckvt_n, p.astype(BF16), preferred_element_type=F32)
    s_n = m_n + 1.0

    s_f = jnp.clip(s_n, 1.0, SHIFT_MAX).astype(BF16).astype(F32)
    set_shift_rows(-s_f)
    acc_ref[...] = jnp.zeros(acc_ref.shape, F32)

    def far_body(j, l_f):
        ckv_t, ckvt_t, keep = far_tile(j)
        p = jnp.exp2(qk(ckv_t, keep))
        acc_ref[...] += jnp.dot(ckvt_t, p.astype(BF16), preferred_element_type=F32)
        return l_f + jnp.sum(p, axis=0, keepdims=True)

    l_f = lax.fori_loop(0, n_far, far_body, jnp.zeros((1, H * tq), F32))

    has_far = n_far > 0
    s_max = jnp.maximum(s_n, s_f)
    w_n = jnp.where(has_far, jnp.exp2(s_n - s_max), 1.0)
    w_f = jnp.where(has_far, jnp.exp2(s_f - s_max), 0.0)
    l_tot = l_n * w_n + l_f * w_f
    acc_ref[...] = (accn_ref[...] * w_n + acc_ref[...] * w_f) * (1.0 / l_tot)
    in_range = (l_tot > L_TOT_MIN) & (l_tot < L_TOT_MAX)
    n_bad = jnp.sum(jnp.where(in_range, 0.0, 1.0))

    @pl.when(n_bad > 0.0)
    def _():
        set_shift_rows(jnp.full((1, H * tq), -1.0, F32))
        m_ref[...] = jnp.full(m_ref.shape, M_INIT, F32)
        l_ref[...] = jnp.zeros(l_ref.shape, F32)
        acc_ref[...] = jnp.zeros(acc_ref.shape, F32)

        def attend(ckv_t, ckvt_t, keep, bias):
            lg = qk(ckv_t, keep)
            if bias is not None:
                lg = lg + bias
            m_prev = m_ref[...]
            m_new = jnp.maximum(m_prev, jnp.max(lg, axis=0, keepdims=True))
            alpha = jnp.exp2(m_prev - m_new)
            p = jnp.exp2(lg - m_new)
            l_ref[...] = alpha * l_ref[...] + jnp.sum(p, axis=0, keepdims=True)
            m_ref[...] = m_new
            pv = jnp.dot(ckvt_t, p.astype(BF16), preferred_element_type=F32)
            acc_ref[...] = alpha * acc_ref[...] + pv

        def slow_body(j, carry):
            attend(*far_tile(j), None)
            return carry

        lax.fori_loop(0, n_far, slow_body, 0)
        attend(*near_tile(), tbl_ref[first])
        acc_ref[...] = acc_ref[...] * (1.0 / l_ref[...])

    ot = acc_ref[...].astype(BF16)
    o_st = jnp.concatenate([ot[:, h * tq:(h + 1) * tq] for h in range(H)], axis=0)
    yt = jnp.dot(wuvt_ref[...], o_st, preferred_element_type=F32)
    o_ref[...] = yt.T.astype(BF16)


def _dsa(qlt, iqt, iwt, ik, ckv, ckvt, tbl, wuvt, layer, B, S):
    tq = Q_BLOCK
    nb = S // tq
    RL = A_HEADS * KV_RANK
    QI = IDX_HEADS * IDX_DIM
    k_sel = min(TOPK_MAX, S // 4)
    assert IDX_TK % k_sel == 0 and k_sel % SUBLANES == 0
    return pl.pallas_call(
        functools.partial(_dsa_kernel, k_sel=k_sel, seq=S),
        out_shape=jax.ShapeDtypeStruct((B * S, A_WIDTH), BF16),
        grid=(B, nb),
        in_specs=[
            pl.BlockSpec((None, RL, tq), lambda b, i: (b, 0, i)),
            pl.BlockSpec((None, QI, tq), lambda b, i: (b, 0, i)),
            pl.BlockSpec((None, IDX_HEADS, tq), lambda b, i: (b, 0, i)),
            pl.BlockSpec((None, S, IDX_DIM), lambda b, i: (b, 0, 0)),
            pl.BlockSpec((None, S, KV_RANK), lambda b, i: (b, 0, 0)),
            pl.BlockSpec((None, KV_RANK, S), lambda b, i: (b, 0, 0)),
            pl.BlockSpec((2, NEAR_W, A_HEADS * tq), lambda b, i: (0, 0, 0)),
            pl.BlockSpec((None, A_WIDTH, RL), lambda b, i: (layer, 0, 0)),
        ],
        out_specs=pl.BlockSpec((tq, A_WIDTH), lambda b, i: (b * nb + i, 0)),
        scratch_shapes=[
            pltpu.VMEM((S, tq), F32),
            pltpu.VMEM((S, tq), F32),
            pltpu.VMEM((SORT_N * SUBLANES, tq), F32),
            pltpu.VMEM((1, tq), jnp.int32),
            pltpu.VMEM((KV_RANK + tq, A_HEADS * tq), BF16),
            pltpu.VMEM((IDX_DIM, IDX_HEADS * tq), BF16),
            pltpu.VMEM((1, A_HEADS * tq), F32),
            pltpu.VMEM((1, A_HEADS * tq), F32),
            pltpu.VMEM((KV_RANK, A_HEADS * tq), F32),
            pltpu.VMEM((KV_RANK, A_HEADS * tq), F32),
            pltpu.VMEM((k_sel, tq), F32),
        ],
        compiler_params=_params("parallel", "arbitrary"),
        name="dsa",
    )(qlt, iqt, iwt, ik, ckv, ckvt, tbl, wuvt)


def _memkv_kernel(mem_ref, g_ref, w_ref, kbd_ref, vbd_ref):
    M = mem_ref.shape[0]
    mn = _rms(mem_ref[...], g_ref[...]).astype(BF16)
    kv = jnp.dot(mn, w_ref[...], preferred_element_type=F32)
    kt = kv[:, :C_WIDTH].T
    v = kv[:, C_WIDTH:]
    r = lax.broadcasted_iota(jnp.int32, (C_WIDTH, C_HEADS * M), 0) // C_HEAD_DIM
    c = lax.broadcasted_iota(jnp.int32, (C_WIDTH, C_HEADS * M), 1) // M
    kbd_ref[...] = jnp.where(r == c, jnp.concatenate([kt] * C_HEADS, axis=1), 0.0).astype(BF16)
    r = lax.broadcasted_iota(jnp.int32, (C_HEADS * M, C_WIDTH), 0) // M
    c = lax.broadcasted_iota(jnp.int32, (C_HEADS * M, C_WIDTH), 1) // C_HEAD_DIM
    vbd_ref[...] = jnp.where(r == c, jnp.concatenate([v] * C_HEADS, axis=0), 0.0).astype(BF16)


def _memkv(mem, g, w, layer):
    B, M, D = mem.shape
    return pl.pallas_call(
        _memkv_kernel,
        out_shape=[jax.ShapeDtypeStruct((B, C_WIDTH, C_HEADS * M), BF16),
                   jax.ShapeDtypeStruct((B, C_HEADS * M, C_WIDTH), BF16)],
        grid=(B,),
        in_specs=[pl.BlockSpec((None, M, D), lambda b: (b, 0, 0)),
                  pl.BlockSpec((None, 1, D), lambda b: (layer, 0, 0)),
                  pl.BlockSpec((None, D, 2 * C_WIDTH), lambda b: (layer, 0, 0))],
        out_specs=[pl.BlockSpec((None, C_WIDTH, C_HEADS * M), lambda b: (b, 0, 0)),
                   pl.BlockSpec((None, C_HEADS * M, C_WIDTH), lambda b: (b, 0, 0))],
        compiler_params=_params("parallel"),
        name="memkv",
    )(mem, g, w)


def _mix_kernel(h_ref, g_ref, wg_ref, ya_ref, z_ref, zh_ref, qc_ref, kbd_ref, vbd_ref, wpool_ref, ps_ref,
                wa_ref, wb_ref, wc_ref, wo_ref, o_ref, zx_ref):
    tm, D = h_ref.shape
    i = pl.program_id(1)
    M = kbd_ref.shape[1] // C_HEADS
    h = h_ref[...]
    u = _rms(h, g_ref[...]).astype(BF16)

    z = z_ref[...]
    zx_ref[0:POOL_HALO, :] = jnp.where(i == 0, 0.0, zh_ref[...])
    zx_ref[POOL_HALO:, :] = z
    pos = i * tm + lax.broadcasted_iota(jnp.int32, (tm, 1), 0) + 1
    grp = lax.broadcasted_iota(jnp.int32, (1, POOL_WIDTH), 1) // POOL_GROUP
    win = z
    pooled = jnp.zeros_like(z)
    k = 1
    for gi, w in enumerate(POOL_WINDOWS):
        while k < w:
            win = win + zx_ref[POOL_HALO - k:POOL_HALO - k + tm, :]
            k += 1
        cnt = jnp.minimum(pos, w).astype(F32)
        pooled = jnp.where(grp == gi, win / cnt - z, pooled)
    yb = jnp.dot(pooled.astype(BF16), wpool_ref[...], preferred_element_type=F32) * ps_ref[...]

    lg = jnp.dot(qc_ref[...], kbd_ref[...], preferred_element_type=F32)
    ps = []
    for hc in range(C_HEADS):
        seg = lg[:, hc * M:(hc + 1) * M]
        e = jnp.exp(seg - jnp.max(seg, axis=-1, keepdims=True))
        ps.append(e / jnp.sum(e, axis=-1, keepdims=True))
    p = jnp.concatenate(ps, axis=1).astype(BF16)
    yc = jnp.dot(p, vbd_ref[...], preferred_element_type=F32)

    def gate(n):
        pre = jnp.dot(u, wg_ref[:, n * D:(n + 1) * D], preferred_element_type=F32)
        return jax.nn.sigmoid(pre)

    merged = gate(0) * jnp.dot(ya_ref[...], wa_ref[...], preferred_element_type=F32)
    merged += gate(1) * jnp.dot(yb.astype(BF16), wb_ref[...], preferred_element_type=F32)
    merged += gate(2) * jnp.dot(yc.astype(BF16), wc_ref[...], preferred_element_type=F32)
    o_ref[...] = h + jnp.dot(merged.astype(BF16), wo_ref[...], preferred_element_type=F32)


def _mix(h, g, ya, z, qc, kbd, vbd, wp, layer, B, S):
    T, D = h.shape
    tm = min(MIX_TM, S)
    nt = S // tm
    hb = tm // POOL_HALO

    def wspec(arr):
        return pl.BlockSpec((None,) + arr.shape[1:], lambda b, i: (layer,) + (0,) * (arr.ndim - 1))

    tok = lambda w: pl.BlockSpec((tm, w), lambda b, i: (b * nt + i, 0))
    names = ("wpool", "ps", "wa", "wb", "wc", "wo")
    return pl.pallas_call(
        _mix_kernel,
        out_shape=jax.ShapeDtypeStruct((T, D), F32),
        grid=(B, nt),
        in_specs=[tok(D), wspec(g), wspec(wp["wg"]), tok(A_WIDTH), tok(POOL_WIDTH),
                  pl.BlockSpec((POOL_HALO, POOL_WIDTH), lambda b, i: (jnp.maximum((b * nt + i) * hb - 1, 0), 0)),
                  tok(C_WIDTH),
                  pl.BlockSpec((None,) + kbd.shape[1:], lambda b, i: (b, 0, 0)),
                  pl.BlockSpec((None,) + vbd.shape[1:], lambda b, i: (b, 0, 0))]
                 + [wspec(wp[k]) for k in names],
        out_specs=tok(D),
        scratch_shapes=[pltpu.VMEM((tm + POOL_HALO, POOL_WIDTH), F32)],
        compiler_params=_params("parallel", "parallel"),
        name="mix",
    )(h, g, wp["wg"], ya, z, z, qc, kbd, vbd, *[wp[k] for k in names])


def _block_diag(blocks):
    L, G, r, c = blocks.shape
    eye = jnp.eye(G, dtype=blocks.dtype)
    return jnp.einsum("lgrc,gh->lgrhc", blocks, eye).reshape(L, G * r, G * c)


def _prep_weights(w_in, kv_norm, w_uk, w_uv, w_pool, pool_scale, w_branch_a, w_branch_b, w_branch_c, w_out):
    L, D, _ = w_in.shape
    sizes = (A_WIDTH, KV_RANK, IDX_HEADS * IDX_DIM, IDX_DIM, IDX_HEADS, POOL_WIDTH, C_WIDTH, N_BRANCH * D)
    offs = np.concatenate([[0], np.cumsum(sizes)])
    wq, wckv, wiq, wik, wiw, wz, wqc, wg = [w_in[:, :, offs[n]:offs[n + 1]] for n in range(len(sizes))]
    proj = dict(
        wq=wq.astype(BF16),
        wukt=_block_diag(jnp.transpose(w_uk, (0, 2, 1, 3))).astype(BF16),
        wckv=wckv.astype(BF16),
        kvg=kv_norm[:, None, :],
        wiqt=jnp.transpose(wiq, (0, 2, 1)).astype(BF16),
        wik=jnp.pad(wik, ((0, 0), (0, 0), (0, LANES - IDX_DIM))).astype(BF16),
        wiwt=jnp.pad(jnp.transpose(wiw, (0, 2, 1)), ((0, 0), (0, BF16_ROWS - IDX_HEADS), (0, 0))).astype(BF16),
        wz=wz.astype(BF16),
        wqc=wqc.astype(BF16),
    )
    wuvt = _block_diag(jnp.transpose(w_uv, (0, 2, 3, 1))).astype(BF16)
    mix = dict(
        wg=wg.astype(BF16),
        wpool=_block_diag(w_pool).astype(BF16),
        ps=pool_scale[:, None, :],
        wa=w_branch_a.astype(BF16), wb=w_branch_b.astype(BF16), wc=w_branch_c.astype(BF16),
        wo=w_out.astype(BF16),
    )
    return proj, wuvt, mix


def kernel(x, mem, rel_bias, ffn1_norm, ffn1_w_in, ffn1_w_out, mix_norm, w_in, kv_norm, w_uk, w_uv, w_pool,
           pool_scale, mem_norm, w_mem_kv, w_branch_a, w_branch_b, w_branch_c, w_out, ffn2_norm, ffn2_w_in,
           ffn2_w_out, final_norm):
    B, S, D = x.shape
    L = w_in.shape[0]
    assert S % Q_BLOCK == 0 and S % IDX_TK == 0 and S % ATT_TK == 0 and S >= NEAR_W

    proj_w, wuvt, mix_w = _prep_weights(w_in, kv_norm, w_uk, w_uv, w_pool, pool_scale,
                                        w_branch_a, w_branch_b, w_branch_c, w_out)
    f1_in, f1_out = ffn1_w_in.astype(BF16), ffn1_w_out.astype(BF16)
    f2_in, f2_out = ffn2_w_in.astype(BF16), ffn2_w_out.astype(BF16)
    f1_g, f2_g, mx_g, mm_g = (a[:, None, :] for a in (ffn1_norm, ffn2_norm, mix_norm, mem_norm))
    fg = final_norm[None, :]
    wmem = w_mem_kv.astype(BF16)
    tbl = _bias_tables(rel_bias)

    h = x.reshape(B * S, D)
    for l in range(L):
        h = _ffn(h, f1_g, f1_in, f1_out, fg, l, final=False)
        qlt, ckv, ckvt, iqt, ik, iwt, z, qc = _proj(h, mx_g, proj_w, l, B, S)
        ya = _dsa(qlt, iqt, iwt, ik.reshape(B, S, IDX_DIM), ckv.reshape(B, S, KV_RANK), ckvt, tbl, wuvt, l, B, S)
        kbd, vbd = _memkv(mem, mm_g, wmem, l)
        h = _mix(h, mx_g, ya, z, qc, kbd, vbd, mix_w, l, B, S)
        h = _ffn(h, f2_g, f2_in, f2_out, fg, l, final=(l == L - 1))
    return h.reshape(B, S, D)
```

```python
import functools
import math

import numpy as np
import jax
import jax.numpy as jnp
from jax import lax
from jax.experimental import pallas as pl
from jax.experimental.pallas import tpu as pltpu

F32 = jnp.float32
BF16 = jnp.bfloat16

A_HEADS = 8
A_HEAD_DIM = 64
A_WIDTH = A_HEADS * A_HEAD_DIM
KV_RANK = 128
IDX_HEADS = 8
IDX_DIM = 32
TOPK_MAX = 256
Q_BLOCK = 128
POOL_WINDOWS = (2, 4, 8, 16)
POOL_GROUP = 64
POOL_WIDTH = len(POOL_WINDOWS) * POOL_GROUP
POOL_HALO = max(POOL_WINDOWS)
C_HEADS = 4
C_HEAD_DIM = 64
C_WIDTH = C_HEADS * C_HEAD_DIM
N_BRANCH = 3
REL_BUCKETS = 32
REL_MAX_DIST = 128
EPS = 1e-6
NEG = -1e30
M_INIT = -1e29
LOG2E = math.log2(math.e)
MASK_BIG = -NEG
SHIFT_MAX = 2.0 ** 20
L_TOT_MIN = 2.0 ** -60
L_TOT_MAX = 2.0 ** 100

LANES = 128
SUBLANES = 8
BF16_ROWS = 16
VMEM_LIMIT = 52 * 1024 * 1024
FFN_TM = 512
PROJ_TM = 512
MIX_TM = 512
IDX_TK = 512
ATT_TK = 512
FAR_UNROLLS = (4, 2, 1)
NEAR_W = 2 * Q_BLOCK
COUNT_PARTIALS = 8
SORT_N = 16
LAYER_ROWS = IDX_TK // SORT_N
LAYER_FULL_STEPS = 5
VALUE_MID_STEPS = 12
BISECT_FIXED_STEPS = 5
KEY_MIN_NORMAL = 0x00800000
F32_BIG = 3.0e38
BISECT_STEPS_PER_TEST = 4


def _oddeven_merge_sort(n):
    def merge(lo, hi, r):
        step = r * 2
        if step < hi - lo:
            yield from merge(lo, hi, step)
            yield from merge(lo + r, hi, step)
            yield from ((i, i + r) for i in range(lo + r, hi - r, step))
        else:
            yield (lo, lo + r)

    def sort(lo, hi):
        if hi - lo >= 1:
            mid = lo + (hi - lo) // 2
            yield from sort(lo, mid)
            yield from sort(mid + 1, hi)
            yield from merge(lo, hi, 1)

    return tuple(sort(0, n - 1))


_SORT_NETWORK = _oddeven_merge_sort(SORT_N)

INT_MIN = -2 ** 31


def _rms(x, g):
    return x * lax.rsqrt(jnp.mean(x * x, axis=-1, keepdims=True) + EPS) * g


def _params(*sem):
    return pltpu.CompilerParams(dimension_semantics=sem, vmem_limit_bytes=VMEM_LIMIT)


def _dot_nt(a, b):
    return lax.dot_general(a, b, (((1,), (1,)), ((), ())), preferred_element_type=F32)


def _ffn_kernel(x_ref, g_ref, wa_ref, wb_ref, wo_ref, fg_ref, o_ref, xn_ref, acc_ref, *, n_ff, final):
    j = pl.program_id(1)

    @pl.when(j == 0)
    def _():
        xn_ref[...] = _rms(x_ref[...], g_ref[...]).astype(BF16)
        acc_ref[...] = jnp.zeros_like(acc_ref)

    xn = xn_ref[...]
    a = jnp.dot(xn, wa_ref[...], preferred_element_type=F32)
    b = jnp.dot(xn, wb_ref[...], preferred_element_type=F32)
    act = (a * jax.nn.sigmoid(a) * b).astype(BF16)
    acc_ref[...] += jnp.dot(act, wo_ref[...], preferred_element_type=F32)

    @pl.when(j == n_ff - 1)
    def _():
        y = x_ref[...] + 0.5 * acc_ref[...]
        if final:
            y = _rms(y, fg_ref[...])
        o_ref[...] = y


def _ffn(h, g, w_in, w_out, fg, layer, *, final):
    T, D = h.shape
    F = w_out.shape[1]
    n_ff = 2 if F % (2 * LANES) == 0 else 1
    fc = F // n_ff
    tm = min(FFN_TM, T)
    return pl.pallas_call(
        functools.partial(_ffn_kernel, n_ff=n_ff, final=final),
        out_shape=jax.ShapeDtypeStruct((T, D), F32),
        grid=(T // tm, n_ff),
        in_specs=[
            pl.BlockSpec((tm, D), lambda i, j: (i, 0)),
            pl.BlockSpec((None, 1, D), lambda i, j: (layer, 0, 0)),
            pl.BlockSpec((None, D, fc), lambda i, j: (layer, 0, j)),
            pl.BlockSpec((None, D, fc), lambda i, j: (layer, 0, n_ff + j)),
            pl.BlockSpec((None, fc, D), lambda i, j: (layer, j, 0)),
            pl.BlockSpec((1, D), lambda i, j: (0, 0)),
        ],
        out_specs=pl.BlockSpec((tm, D), lambda i, j: (i, 0)),
        scratch_shapes=[pltpu.VMEM((tm, D), BF16), pltpu.VMEM((tm, D), F32)],
        compiler_params=_params("parallel", "arbitrary"),
        name="ffn",
    )(h, g, w_in, w_in, w_out, fg)


_PROJ_W = ("wq", "wukt", "wckv", "kvg", "wiqt", "wik", "wiwt", "wz", "wqc")


def _proj_kernel(h_ref, g_ref, wq_ref, wukt_ref, wckv_ref, kvg_ref, wiqt_ref, wik_ref, wiwt_ref, wz_ref, wqc_ref,
                 qlt_ref, ckv_ref, ckvt_ref, iqt_ref, ik_ref, iwt_ref, z_ref, qc_ref):
    u = _rms(h_ref[...], g_ref[...]).astype(BF16)
    q = jnp.dot(u, wq_ref[...], preferred_element_type=F32).astype(BF16)
    qlt_ref[...] = (_dot_nt(wukt_ref[...], q) * (A_HEAD_DIM ** -0.5 * LOG2E)).astype(BF16)
    c = _rms(jnp.dot(u, wckv_ref[...], preferred_element_type=F32), kvg_ref[...])
    ckv_ref[...] = c.astype(BF16)
    ckvt_ref[...] = c.T.astype(BF16)
    iqt_ref[...] = _dot_nt(wiqt_ref[...], u).astype(BF16)
    ik_ref[...] = jnp.dot(u, wik_ref[...], preferred_element_type=F32)[:, :IDX_DIM].astype(BF16)
    iwt = _dot_nt(wiwt_ref[...], u)[:IDX_HEADS]
    iwt_ref[...] = iwt * ((IDX_DIM ** -0.5) * (IDX_HEADS ** -0.5))
    z_ref[...] = jnp.dot(u, wz_ref[...], preferred_element_type=F32)
    qc = jnp.dot(u, wqc_ref[...], preferred_element_type=F32) * (C_HEAD_DIM ** -0.5)
    qc_ref[...] = qc.astype(BF16)


def _proj(h, g, wp, layer, B, S):
    T, D = h.shape
    tm = min(PROJ_TM, S)
    nt = S // tm
    RL = A_HEADS * KV_RANK
    QI = IDX_HEADS * IDX_DIM

    def wspec(arr):
        return pl.BlockSpec((None,) + arr.shape[1:], lambda b, i: (layer,) + (0,) * (arr.ndim - 1))

    tok = lambda w: pl.BlockSpec((tm, w), lambda b, i: (b * nt + i, 0))
    tokt = lambda w: pl.BlockSpec((None, w, tm), lambda b, i: (b, 0, i))
    return pl.pallas_call(
        _proj_kernel,
        out_shape=[
            jax.ShapeDtypeStruct((B, RL, S), BF16),
            jax.ShapeDtypeStruct((T, KV_RANK), BF16),
            jax.ShapeDtypeStruct((B, KV_RANK, S), BF16),
            jax.ShapeDtypeStruct((B, QI, S), BF16),
            jax.ShapeDtypeStruct((T, IDX_DIM), BF16),
            jax.ShapeDtypeStruct((B, IDX_HEADS, S), F32),
            jax.ShapeDtypeStruct((T, POOL_WIDTH), F32),
            jax.ShapeDtypeStruct((T, C_WIDTH), BF16),
        ],
        grid=(B, nt),
        in_specs=[tok(D), wspec(g)] + [wspec(wp[k]) for k in _PROJ_W],
        out_specs=[tokt(RL), tok(KV_RANK), tokt(KV_RANK), tokt(QI), tok(IDX_DIM), tokt(IDX_HEADS),
                   tok(POOL_WIDTH), tok(C_WIDTH)],
        compiler_params=_params("parallel", "parallel"),
        name="proj",
    )(h, g, *[wp[k] for k in _PROJ_W])


def _t5_bucket_np(dist):
    max_exact = REL_BUCKETS // 2
    n = np.maximum(dist, 0)
    nf = np.maximum(n, 1).astype(np.float32)
    large = max_exact + (np.log(nf / np.float32(max_exact)) / np.float32(math.log(REL_MAX_DIST / max_exact))
                         * np.float32(REL_BUCKETS - max_exact)).astype(np.int32)
    large = np.minimum(large, REL_BUCKETS - 1)
    return np.where(n < max_exact, n, large).astype(np.int32)


def _bias_kernel(rb_ref, bucket_ref, o_ref):
    for k in range(2):
        bk = bucket_ref[k]
        for h in range(A_HEADS):
            far = rb_ref[REL_BUCKETS - 1, h]
            acc = jnp.zeros(bk.shape, F32)
            for b in range(REL_BUCKETS - 1):
                acc = jnp.where(bk == b, (rb_ref[b, h] - far) * LOG2E, acc)
            o_ref[k, :, h * Q_BLOCK:(h + 1) * Q_BLOCK] = acc


def _bias_tables(rel_bias):
    c = np.arange(NEAR_W)[:, None]
    r = np.arange(Q_BLOCK)[None, :]
    buckets = np.stack([_t5_bucket_np(r + Q_BLOCK - c), _t5_bucket_np(r - c)])
    return pl.pallas_call(
        _bias_kernel,
        out_shape=jax.ShapeDtypeStruct((2, NEAR_W, A_HEADS * Q_BLOCK), F32),
        in_specs=[pl.BlockSpec(memory_space=pltpu.SMEM), pl.BlockSpec(memory_space=pltpu.VMEM)],
        out_specs=pl.BlockSpec(memory_space=pltpu.VMEM),
        name="bias_tables",
    )(rel_bias, jnp.asarray(buckets))


def _key_to_f32(k):
    return lax.bitcast_convert_type(jnp.where(k < 0, INT_MIN - k, k), F32)


def _f32_to_key(x):
    b = lax.bitcast_convert_type(x, jnp.int32)
    return jnp.where(b < 0, INT_MIN - b, b)


def _dsa_kernel(qlt_ref, iqt_ref, iwt_ref, ik_ref, ckv_ref, ckvt_ref, tbl_ref, wuvt_ref, o_ref,
                sc_ref, lay_ref, lmax_ref, thr_ref, qs_ref, iqs_ref, m_ref, l_ref, acc_ref, accn_ref, fold_ref,
                *, k_sel, seq):
    tq = Q_BLOCK
    H = A_HEADS
    i = pl.program_id(1)
    row0 = i * tq
    n_idx = (row0 + tq + IDX_TK - 1) // IDX_TK

    eye_b = lax.broadcasted_iota(jnp.int32, (tq, tq), 0) == lax.broadcasted_iota(jnp.int32, (tq, tq), 1)
    for h in range(H):
        qs_ref[0:KV_RANK, h * tq:(h + 1) * tq] = qlt_ref[h * KV_RANK:(h + 1) * KV_RANK, :]
        iqs_ref[:, h * tq:(h + 1) * tq] = iqt_ref[h * IDX_DIM:(h + 1) * IDX_DIM, :]

    w_rows = [iwt_ref[h:h + 1, :] for h in range(IDX_HEADS)]
    qid = row0 + lax.broadcasted_iota(jnp.int32, (IDX_TK, tq), 1)
    kid0 = lax.broadcasted_iota(jnp.int32, (IDX_TK, tq), 0)
    fold_ref[...] = jnp.full(fold_ref.shape, -jnp.inf, F32)
    lmax_ref[...] = jnp.full(lmax_ref.shape, -jnp.inf, F32)

    def idx_tile(j, on_diagonal):
        r0 = pl.multiple_of(j * IDX_TK, IDX_TK)
        d = jnp.dot(ik_ref[pl.ds(r0, IDX_TK), :], iqs_ref[...], preferred_element_type=F32)
        s = jnp.maximum(d[:, 0:tq], 0.0) * w_rows[0]
        for h in range(1, IDX_HEADS):
            s = s + jnp.maximum(d[:, h * tq:(h + 1) * tq], 0.0) * w_rows[h]
        if on_diagonal:
            s = jnp.where(kid0 + r0 <= qid, s, -jnp.inf)
        sc_ref[pl.ds(r0, IDX_TK), :] = s
        f = s[0:k_sel]
        for q in range(1, IDX_TK // k_sel):
            f = jnp.maximum(f, s[q * k_sel:(q + 1) * k_sel])
        fold_ref[...] = jnp.maximum(fold_ref[...], f)
        n_grp = IDX_TK // (SORT_N * SUBLANES)
        x = [jnp.concatenate([s[(g * SORT_N + r) * SUBLANES:(g * SORT_N + r + 1) * SUBLANES]
                              for g in range(n_grp)], axis=0) for r in range(SORT_N)]
        for a, b in _SORT_NETWORK:
            x[a], x[b] = jnp.maximum(x[a], x[b]), jnp.minimum(x[a], x[b])
        for r in range(SORT_N):
            lay_ref[pl.ds(r0 + r * LAYER_ROWS, LAYER_ROWS), :] = x[r]
            m = x[r][0:SUBLANES]
            for g in range(1, n_grp):
                m = jnp.maximum(m, x[r][g * SUBLANES:(g + 1) * SUBLANES])
            lmax_ref[r * SUBLANES:(r + 1) * SUBLANES, :] = jnp.maximum(lmax_ref[r * SUBLANES:(r + 1) * SUBLANES, :], m)

    def idx_pair(j2, carry):
        idx_tile(2 * j2, False)
        idx_tile(2 * j2 + 1, False)
        return carry

    def idx_single(j, carry):
        idx_tile(j, False)
        return carry

    n_below = n_idx - 1
    lax.fori_loop(0, n_below // 2, idx_pair, 0)
    lax.fori_loop(2 * (n_below // 2), n_below, idx_single, 0)
    idx_tile(n_below, True)

    n_par = COUNT_PARTIALS

    def count(pred):
        def body(j, acc):
            r0 = pl.multiple_of(j * IDX_TK, IDX_TK)
            ind = jnp.where(pred(sc_ref[pl.ds(r0, IDX_TK), :], r0), 1.0, 0.0)
            return acc + jnp.sum(ind.reshape(n_par, IDX_TK // n_par, tq), axis=0)
        acc = lax.fori_loop(0, n_idx, body, jnp.zeros((IDX_TK // n_par, tq), F32))
        return jnp.sum(acc, axis=0, keepdims=True)

    kf = float(k_sel)
    fold = fold_ref[...]
    lo0 = _f32_to_key(jnp.min(fold, axis=0, keepdims=True))
    hi0 = _f32_to_key(jnp.max(fold, axis=0, keepdims=True))

    def n_open(lo, hi):
        return jnp.sum((lo < hi).astype(F32))

    def run_bisection(count_fn, lo, hi):
        for value_mid, steps in ((True, VALUE_MID_STEPS - LAYER_FULL_STEPS), (False, BISECT_FIXED_STEPS)):
            step = functools.partial(bis_step, count_fn, value_mid)
            lo, hi = lax.fori_loop(0, steps, lambda _, st: step(*st), (lo, hi))

        def body(st):
            lo_n, hi_n = lax.fori_loop(0, BISECT_STEPS_PER_TEST, lambda _, s2: step(*s2), (st[0], st[1]))
            return lo_n, hi_n, n_open(lo_n, hi_n)

        lo, hi, _ = lax.while_loop(lambda st: st[2] > 0.0, body, (lo, hi, n_open(lo, hi)))
        return lo

    def bis_step(count_fn, value_mid, lo, hi):
        d = hi - lo
        mid = lo + lax.shift_right_logical(d, 1) + (d & 1)
        if value_mid:
            lo_f = jnp.maximum(_key_to_f32(lo), -F32_BIG)
            hi_f = jnp.minimum(_key_to_f32(hi), F32_BIG)
            wide = (hi_f - lo_f) > 0.5 * jnp.maximum(jnp.abs(lo_f), jnp.abs(hi_f))
            vmid = jnp.minimum(jnp.maximum(_f32_to_key(0.5 * lo_f + 0.5 * hi_f), lo + 1), hi)
            mid = jnp.where(wide, vmid, mid)
        at_zero = (lo == 0) & (hi >= KEY_MIN_NORMAL)
        mid = jnp.where((lo < 0) & (hi >= 0), 0, jnp.where(at_zero, KEY_MIN_NORMAL, mid))
        thr = _key_to_f32(mid)
        cnt = count_fn(lambda s, r0: s >= thr)
        active = lo < hi
        ge = cnt >= kf
        lo_n = jnp.where(active & ge, mid, lo)
        hi_lt = jnp.where(at_zero, 0, mid - 1)
        hi_n = jnp.where(active, jnp.where(ge, jnp.where(cnt == kf, mid, hi), hi_lt), hi)
        return lo_n, hi_n

    lo, hi = lax.fori_loop(0, LAYER_FULL_STEPS, lambda _, st: bis_step(count, True, *st), (lo0, hi0))
    layer_max = [jnp.max(lmax_ref[r * SUBLANES:(r + 1) * SUBLANES, :], axis=0, keepdims=True)
                 for r in range(SORT_N)]
    lo_f = _key_to_f32(lo)
    n_alive = sum(jnp.max(jnp.where(layer_max[r] >= lo_f, 1.0, 0.0)) for r in range(SORT_N))
    n_quarters = (n_alive.astype(jnp.int32) + SORT_N // 4 - 1) // (SORT_N // 4)

    for k in range(1, 5):
        rows = k * (SORT_N // 4) * LAYER_ROWS

        def count_top(pred, rows=rows):
            def body(j, acc):
                r0 = pl.multiple_of(j * IDX_TK, IDX_TK)
                ind = jnp.where(pred(lay_ref[pl.ds(r0, rows), :], r0), 1.0, 0.0)
                return acc + jnp.sum(ind.reshape(n_par, rows // n_par, tq), axis=0)
            acc = lax.fori_loop(0, n_idx, body, jnp.zeros((rows // n_par, tq), F32))
            return jnp.sum(acc, axis=0, keepdims=True)

        cond = (n_quarters <= 1) if k == 1 else ((n_quarters >= 4) if k == 4 else (n_quarters == k))

        @pl.when(cond)
        def _(count_top=count_top):
            thr_ref[...] = run_bisection(count_top, lo, hi)

    lo = thr_ref[...]
    thr = _key_to_f32(lo)

    cnt_ge = count(lambda s, r0: s >= thr)
    n_tie = jnp.sum((cnt_ge > kf).astype(F32))

    @pl.when(n_tie > 0.0)
    def _():
        need = kf - count(lambda s, r0: s > thr)

        def pos_body(_, st):
            plo, phi = st
            mid = (plo + phi) // 2
            c = count(lambda s, r0: (s == thr) & (kid0 + r0 <= mid))
            ok = c >= need
            return jnp.where(ok, plo, mid + 1), jnp.where(ok, mid, phi)

        plo0 = jnp.zeros((1, tq), jnp.int32)
        phi0 = jnp.full((1, tq), seq - 1, jnp.int32)
        last, _ = lax.fori_loop(0, max(1, (seq - 1).bit_length()), pos_body, (plo0, phi0))

        def drop_body(j, carry):
            r0 = pl.multiple_of(j * IDX_TK, IDX_TK)
            s = sc_ref[pl.ds(r0, IDX_TK), :]
            sc_ref[pl.ds(r0, IDX_TK), :] = jnp.where((s == thr) & (kid0 + r0 > last), -jnp.inf, s)
            return carry

        lax.fori_loop(0, n_idx, drop_body, 0)

    def set_shift_rows(neg_shift):
        for h in range(H):
            blk = jnp.where(eye_b, neg_shift[:, h * tq:(h + 1) * tq], 0.0)
            qs_ref[KV_RANK:, h * tq:(h + 1) * tq] = blk.astype(BF16)

    def qk(ckv_t, keep):
        lhs = jnp.concatenate([ckv_t, jnp.where(keep, 1.0, MASK_BIG).astype(BF16)], axis=1)
        return jnp.dot(lhs, qs_ref[...], preferred_element_type=F32)

    near0 = jnp.maximum(row0 - tq, 0)
    n_far = (near0 + ATT_TK - 1) // ATT_TK
    fkid0 = lax.broadcasted_iota(jnp.int32, (ATT_TK, tq), 0)
    n0 = pl.multiple_of(near0, tq)
    first = (i == 0).astype(jnp.int32)
    nkid = n0 + lax.broadcasted_iota(jnp.int32, (NEAR_W, tq), 0)
    nqid = row0 + lax.broadcasted_iota(jnp.int32, (NEAR_W, tq), 1)

    def far_tile(j):
        r0 = pl.multiple_of(j * ATT_TK, ATT_TK)
        keep = (sc_ref[pl.ds(r0, ATT_TK), :] >= thr) & (fkid0 + r0 < near0)
        return ckv_ref[pl.ds(r0, ATT_TK), :], ckvt_ref[:, pl.ds(r0, ATT_TK)], keep

    def near_tile():
        keep = (sc_ref[pl.ds(n0, NEAR_W), :] >= thr) & (nkid <= nqid)
        return ckv_ref[pl.ds(n0, NEAR_W), :], ckvt_ref[:, pl.ds(n0, NEAR_W)], keep

    set_shift_rows(jnp.full((1, H * tq), -1.0, F32))
    ckv_n, ckvt_n, keep_n = near_tile()
    lg = qk(ckv_n, keep_n) + tbl_ref[first]
    m_n = jnp.maximum(jnp.max(lg, axis=0, keepdims=True), M_INIT)
    p = jnp.exp2(lg - m_n)
    l_n = jnp.sum(p, axis=0, keepdims=True)
    accn_ref[...] = jnp.dot(ckvt_n, p.astype(BF16), preferred_element_type=F32)
    s_n = m_n + 1.0

    s_f = jnp.clip(s_n, 1.0, SHIFT_MAX).astype(BF16).astype(F32)
    set_shift_rows(-s_f)
    acc_ref[...] = jnp.zeros(acc_ref.shape, F32)

    def far_group(first, unroll, l_f):
        tiles = [far_tile(first + u) for u in range(unroll)]
        ps = [jnp.exp2(qk(ckv_t, keep)) for ckv_t, _, keep in tiles]
        pv = jnp.dot(tiles[0][1], ps[0].astype(BF16), preferred_element_type=F32)
        for u in range(1, unroll):
            pv = pv + jnp.dot(tiles[u][1], ps[u].astype(BF16), preferred_element_type=F32)
        acc_ref[...] += pv
        for p in ps:
            l_f = l_f + jnp.sum(p, axis=0, keepdims=True)
        return l_f

    l_f = jnp.zeros((1, H * tq), F32)
    done = 0
    for unroll in FAR_UNROLLS:
        n_grp = (n_far - done) // unroll
        l_f = lax.fori_loop(0, n_grp, lambda g, l, done=done, unroll=unroll:
                            far_group(done + g * unroll, unroll, l), l_f)
        done = done + n_grp * unroll

    has_far = n_far > 0
    s_max = jnp.maximum(s_n, s_f)
    w_n = jnp.where(has_far, jnp.exp2(s_n - s_max), 1.0)
    w_f = jnp.where(has_far, jnp.exp2(s_f - s_max), 0.0)
    l_tot = l_n * w_n + l_f * w_f
    acc_ref[...] = (accn_ref[...] * w_n + acc_ref[...] * w_f) * (1.0 / l_tot)
    in_range = (l_tot > L_TOT_MIN) & (l_tot < L_TOT_MAX)
    n_bad = jnp.sum(jnp.where(in_range, 0.0, 1.0))

    @pl.when(n_bad > 0.0)
    def _():
        set_shift_rows(jnp.full((1, H * tq), -1.0, F32))
        m_ref[...] = jnp.full(m_ref.shape, M_INIT, F32)
        l_ref[...] = jnp.zeros(l_ref.shape, F32)
        acc_ref[...] = jnp.zeros(acc_ref.shape, F32)

        def attend(ckv_t, ckvt_t, keep, bias):
            lg = qk(ckv_t, keep)
            if bias is not None:
                lg = lg + bias
            m_prev = m_ref[...]
            m_new = jnp.maximum(m_prev, jnp.max(lg, axis=0, keepdims=True))
            alpha = jnp.exp2(m_prev - m_new)
            p = jnp.exp2(lg - m_new)
            l_ref[...] = alpha * l_ref[...] + jnp.sum(p, axis=0, keepdims=True)
            m_ref[...] = m_new
            pv = jnp.dot(ckvt_t, p.astype(BF16), preferred_element_type=F32)
            acc_ref[...] = alpha * acc_ref[...] + pv

        def slow_body(j, carry):
            attend(*far_tile(j), None)
            return carry

        lax.fori_loop(0, n_far, slow_body, 0)
        attend(*near_tile(), tbl_ref[first])
        acc_ref[...] = acc_ref[...] * (1.0 / l_ref[...])

    ot = acc_ref[...].astype(BF16)
    o_st = jnp.concatenate([ot[:, h * tq:(h + 1) * tq] for h in range(H)], axis=0)
    yt = jnp.dot(wuvt_ref[...], o_st, preferred_element_type=F32)
    o_ref[...] = yt.T.astype(BF16)


def _dsa(qlt, iqt, iwt, ik, ckv, ckvt, tbl, wuvt, layer, B, S):
    tq = Q_BLOCK
    nb = S // tq
    RL = A_HEADS * KV_RANK
    QI = IDX_HEADS * IDX_DIM
    k_sel = min(TOPK_MAX, S // 4)
    assert IDX_TK % k_sel == 0 and k_sel % SUBLANES == 0
    return pl.pallas_call(
        functools.partial(_dsa_kernel, k_sel=k_sel, seq=S),
        out_shape=jax.ShapeDtypeStruct((B * S, A_WIDTH), BF16),
        grid=(B, nb),
        in_specs=[
            pl.BlockSpec((None, RL, tq), lambda b, i: (b, 0, i)),
            pl.BlockSpec((None, QI, tq), lambda b, i: (b, 0, i)),
            pl.BlockSpec((None, IDX_HEADS, tq), lambda b, i: (b, 0, i)),
            pl.BlockSpec((None, S, IDX_DIM), lambda b, i: (b, 0, 0)),
            pl.BlockSpec((None, S, KV_RANK), lambda b, i: (b, 0, 0)),
            pl.BlockSpec((None, KV_RANK, S), lambda b, i: (b, 0, 0)),
            pl.BlockSpec((2, NEAR_W, A_HEADS * tq), lambda b, i: (0, 0, 0)),
            pl.BlockSpec((None, A_WIDTH, RL), lambda b, i: (layer, 0, 0)),
        ],
        out_specs=pl.BlockSpec((tq, A_WIDTH), lambda b, i: (b * nb + i, 0)),
        scratch_shapes=[
            pltpu.VMEM((S, tq), F32),
            pltpu.VMEM((S, tq), F32),
            pltpu.VMEM((SORT_N * SUBLANES, tq), F32),
            pltpu.VMEM((1, tq), jnp.int32),
            pltpu.VMEM((KV_RANK + tq, A_HEADS * tq), BF16),
            pltpu.VMEM((IDX_DIM, IDX_HEADS * tq), BF16),
            pltpu.VMEM((1, A_HEADS * tq), F32),
            pltpu.VMEM((1, A_HEADS * tq), F32),
            pltpu.VMEM((KV_RANK, A_HEADS * tq), F32),
            pltpu.VMEM((KV_RANK, A_HEADS * tq), F32),
            pltpu.VMEM((k_sel, tq), F32),
        ],
        compiler_params=_params("parallel", "arbitrary"),
        name="dsa",
    )(qlt, iqt, iwt, ik, ckv, ckvt, tbl, wuvt)


def _memkv_kernel(mem_ref, g_ref, w_ref, kbd_ref, vbd_ref):
    M = mem_ref.shape[0]
    mn = _rms(mem_ref[...], g_ref[...]).astype(BF16)
    kv = jnp.dot(mn, w_ref[...], preferred_element_type=F32)
    kt = kv[:, :C_WIDTH].T
    v = kv[:, C_WIDTH:]
    r = lax.broadcasted_iota(jnp.int32, (C_WIDTH, C_HEADS * M), 0) // C_HEAD_DIM
    c = lax.broadcasted_iota(jnp.int32, (C_WIDTH, C_HEADS * M), 1) // M
    kbd_ref[...] = jnp.where(r == c, jnp.concatenate([kt] * C_HEADS, axis=1), 0.0).astype(BF16)
    r = lax.broadcasted_iota(jnp.int32, (C_HEADS * M, C_WIDTH), 0) // M
    c = lax.broadcasted_iota(jnp.int32, (C_HEADS * M, C_WIDTH), 1) // C_HEAD_DIM
    vbd_ref[...] = jnp.where(r == c, jnp.concatenate([v] * C_HEADS, axis=0), 0.0).astype(BF16)


def _memkv(mem, g, w, layer):
    B, M, D = mem.shape
    return pl.pallas_call(
        _memkv_kernel,
        out_shape=[jax.ShapeDtypeStruct((B, C_WIDTH, C_HEADS * M), BF16),
                   jax.ShapeDtypeStruct((B, C_HEADS * M, C_WIDTH), BF16)],
        grid=(B,),
        in_specs=[pl.BlockSpec((None, M, D), lambda b: (b, 0, 0)),
                  pl.BlockSpec((None, 1, D), lambda b: (layer, 0, 0)),
                  pl.BlockSpec((None, D, 2 * C_WIDTH), lambda b: (layer, 0, 0))],
        out_specs=[pl.BlockSpec((None, C_WIDTH, C_HEADS * M), lambda b: (b, 0, 0)),
                   pl.BlockSpec((None, C_HEADS * M, C_WIDTH), lambda b: (b, 0, 0))],
        compiler_params=_params("parallel"),
        name="memkv",
    )(mem, g, w)


def _mix_kernel(h_ref, g_ref, wg_ref, ya_ref, z_ref, zh_ref, qc_ref, kbd_ref, vbd_ref, wpool_ref, ps_ref,
                wa_ref, wb_ref, wc_ref, wo_ref, o_ref, zx_ref):
    tm, D = h_ref.shape
    i = pl.program_id(1)
    M = kbd_ref.shape[1] // C_HEADS
    h = h_ref[...]
    u = _rms(h, g_ref[...]).astype(BF16)

    z = z_ref[...]
    zx_ref[0:POOL_HALO, :] = jnp.where(i == 0, 0.0, zh_ref[...])
    zx_ref[POOL_HALO:, :] = z
    pos = i * tm + lax.broadcasted_iota(jnp.int32, (tm, 1), 0) + 1
    grp = lax.broadcasted_iota(jnp.int32, (1, POOL_WIDTH), 1) // POOL_GROUP
    win = z
    pooled = jnp.zeros_like(z)
    k = 1
    for gi, w in enumerate(POOL_WINDOWS):
        while k < w:
            win = win + zx_ref[POOL_HALO - k:POOL_HALO - k + tm, :]
            k += 1
        cnt = jnp.minimum(pos, w).astype(F32)
        pooled = jnp.where(grp == gi, win / cnt - z, pooled)
    yb = jnp.dot(pooled.astype(BF16), wpool_ref[...], preferred_element_type=F32) * ps_ref[...]

    lg = jnp.dot(qc_ref[...], kbd_ref[...], preferred_element_type=F32)
    ps = []
    for hc in range(C_HEADS):
        seg = lg[:, hc * M:(hc + 1) * M]
        e = jnp.exp(seg - jnp.max(seg, axis=-1, keepdims=True))
        ps.append(e / jnp.sum(e, axis=-1, keepdims=True))
    p = jnp.concatenate(ps, axis=1).astype(BF16)
    yc = jnp.dot(p, vbd_ref[...], preferred_element_type=F32)

    def gate(n):
        pre = jnp.dot(u, wg_ref[:, n * D:(n + 1) * D], preferred_element_type=F32)
        return jax.nn.sigmoid(pre)

    merged = gate(0) * jnp.dot(ya_ref[...], wa_ref[...], preferred_element_type=F32)
    merged += gate(1) * jnp.dot(yb.astype(BF16), wb_ref[...], preferred_element_type=F32)
    merged += gate(2) * jnp.dot(yc.astype(BF16), wc_ref[...], preferred_element_type=F32)
    o_ref[...] = h + jnp.dot(merged.astype(BF16), wo_ref[...], preferred_element_type=F32)


def _mix(h, g, ya, z, qc, kbd, vbd, wp, layer, B, S):
    T, D = h.shape
    tm = min(MIX_TM, S)
    nt = S // tm
    hb = tm // POOL_HALO

    def wspec(arr):
        return pl.BlockSpec((None,) + arr.shape[1:], lambda b, i: (layer,) + (0,) * (arr.ndim - 1))

    tok = lambda w: pl.BlockSpec((tm, w), lambda b, i: (b * nt + i, 0))
    names = ("wpool", "ps", "wa", "wb", "wc", "wo")
    return pl.pallas_call(
        _mix_kernel,
        out_shape=jax.ShapeDtypeStruct((T, D), F32),
        grid=(B, nt),
        in_specs=[tok(D), wspec(g), wspec(wp["wg"]), tok(A_WIDTH), tok(POOL_WIDTH),
                  pl.BlockSpec((POOL_HALO, POOL_WIDTH), lambda b, i: (jnp.maximum((b * nt + i) * hb - 1, 0), 0)),
                  tok(C_WIDTH),
                  pl.BlockSpec((None,) + kbd.shape[1:], lambda b, i: (b, 0, 0)),
                  pl.BlockSpec((None,) + vbd.shape[1:], lambda b, i: (b, 0, 0))]
                 + [wspec(wp[k]) for k in names],
        out_specs=tok(D),
        scratch_shapes=[pltpu.VMEM((tm + POOL_HALO, POOL_WIDTH), F32)],
        compiler_params=_params("parallel", "parallel"),
        name="mix",
    )(h, g, wp["wg"], ya, z, z, qc, kbd, vbd, *[wp[k] for k in names])


def _block_diag(blocks):
    L, G, r, c = blocks.shape
    eye = jnp.eye(G, dtype=blocks.dtype)
    return jnp.einsum("lgrc,gh->lgrhc", blocks, eye).reshape(L, G * r, G * c)


def _prep_weights(w_in, kv_norm, w_uk, w_uv, w_pool, pool_scale, w_branch_a, w_branch_b, w_branch_c, w_out):
    L, D, _ = w_in.shape
    sizes = (A_WIDTH, KV_RANK, IDX_HEADS * IDX_DIM, IDX_DIM, IDX_HEADS, POOL_WIDTH, C_WIDTH, N_BRANCH * D)
    offs = np.concatenate([[0], np.cumsum(sizes)])
    wq, wckv, wiq, wik, wiw, wz, wqc, wg = [w_in[:, :, offs[n]:offs[n + 1]] for n in range(len(sizes))]
    proj = dict(
        wq=wq.astype(BF16),
        wukt=_block_diag(jnp.transpose(w_uk, (0, 2, 1, 3))).astype(BF16),
        wckv=wckv.astype(BF16),
        kvg=kv_norm[:, None, :],
        wiqt=jnp.transpose(wiq, (0, 2, 1)).astype(BF16),
        wik=jnp.pad(wik, ((0, 0), (0, 0), (0, LANES - IDX_DIM))).astype(BF16),
        wiwt=jnp.pad(jnp.transpose(wiw, (0, 2, 1)), ((0, 0), (0, BF16_ROWS - IDX_HEADS), (0, 0))).astype(BF16),
        wz=wz.astype(BF16),
        wqc=wqc.astype(BF16),
    )
    wuvt = _block_diag(jnp.transpose(w_uv, (0, 2, 3, 1))).astype(BF16)
    mix = dict(
        wg=wg.astype(BF16),
        wpool=_block_diag(w_pool).astype(BF16),
        ps=pool_scale[:, None, :],
        wa=w_branch_a.astype(BF16), wb=w_branch_b.astype(BF16), wc=w_branch_c.astype(BF16),
        wo=w_out.astype(BF16),
    )
    return proj, wuvt, mix


def kernel(x, mem, rel_bias, ffn1_norm, ffn1_w_in, ffn1_w_out, mix_norm, w_in, kv_norm, w_uk, w_uv, w_pool,
           pool_scale, mem_norm, w_mem_kv, w_branch_a, w_branch_b, w_branch_c, w_out, ffn2_norm, ffn2_w_in,
           ffn2_w_out, final_norm):
    B, S, D = x.shape
    L = w_in.shape[0]
    assert S % Q_BLOCK == 0 and S % IDX_TK == 0 and S % ATT_TK == 0 and S >= NEAR_W

    proj_w, wuvt, mix_w = _prep_weights(w_in, kv_norm, w_uk, w_uv, w_pool, pool_scale,
                                        w_branch_a, w_branch_b, w_branch_c, w_out)
    f1_in, f1_out = ffn1_w_in.astype(BF16), ffn1_w_out.astype(BF16)
    f2_in, f2_out = ffn2_w_in.astype(BF16), ffn2_w_out.astype(BF16)
    f1_g, f2_g, mx_g, mm_g = (a[:, None, :] for a in (ffn1_norm, ffn2_norm, mix_norm, mem_norm))
    fg = final_norm[None, :]
    wmem = w_mem_kv.astype(BF16)
    tbl = _bias_tables(rel_bias)

    h = x.reshape(B * S, D)
    for l in range(L):
        h = _ffn(h, f1_g, f1_in, f1_out, fg, l, final=False)
        qlt, ckv, ckvt, iqt, ik, iwt, z, qc = _proj(h, mx_g, proj_w, l, B, S)
        ya = _dsa(qlt, iqt, iwt, ik.reshape(B, S, IDX_DIM), ckv.reshape(B, S, KV_RANK), ckvt, tbl, wuvt, l, B, S)
        kbd, vbd = _memkv(mem, mm_g, wmem, l)
        h = _mix(h, mx_g, ya, z, qc, kbd, vbd, mix_w, l, B, S)
        h = _ffn(h, f2_g, f2_in, f2_out, fg, l, final=(l == L - 1))
    return h.reshape(B, S, D)
```

```python
import functools
import math

import numpy as np
import jax
import jax.numpy as jnp
from jax import lax
from jax.experimental import pallas as pl
from jax.experimental.pallas import tpu as pltpu

F32 = jnp.float32
BF16 = jnp.bfloat16

A_HEADS = 8
A_HEAD_DIM = 64
A_WIDTH = A_HEADS * A_HEAD_DIM
KV_RANK = 128
IDX_HEADS = 8
IDX_DIM = 32
TOPK_MAX = 256
Q_BLOCK = 128
POOL_WINDOWS = (2, 4, 8, 16)
POOL_GROUP = 64
POOL_WIDTH = len(POOL_WINDOWS) * POOL_GROUP
POOL_HALO = max(POOL_WINDOWS)
C_HEADS = 4
C_HEAD_DIM = 64
C_WIDTH = C_HEADS * C_HEAD_DIM
N_BRANCH = 3
REL_BUCKETS = 32
REL_MAX_DIST = 128
EPS = 1e-6
NEG = -1e30
M_INIT = -1e29
LOG2E = math.log2(math.e)
MASK_BIG = -NEG
SHIFT_MAX = 2.0 ** 20
L_TOT_MIN = 2.0 ** -60
L_TOT_MAX = 2.0 ** 100

LANES = 128
SUBLANES = 8
BF16_ROWS = 16
VMEM_LIMIT = 52 * 1024 * 1024
FFN_TM = 512
PROJ_TM = 512
MIX_TM = 512
IDX_TK = 512
ATT_TK = 512
FAR_UNROLLS = (4, 2, 1)
NEAR_W = 2 * Q_BLOCK
COUNT_PARTIALS = 8
SORT_N = 16
LAYER_ROWS = IDX_TK // SORT_N
LAYER_FULL_STEPS = 5
VALUE_MID_STEPS = 12
BISECT_FIXED_STEPS = 5
KEY_MIN_NORMAL = 0x00800000
F32_BIG = 3.0e38
BISECT_STEPS_PER_TEST = 4


def _oddeven_merge_sort(n):
    def merge(lo, hi, r):
        step = r * 2
        if step < hi - lo:
            yield from merge(lo, hi, step)
            yield from merge(lo + r, hi, step)
            yield from ((i, i + r) for i in range(lo + r, hi - r, step))
        else:
            yield (lo, lo + r)

    def sort(lo, hi):
        if hi - lo >= 1:
            mid = lo + (hi - lo) // 2
            yield from sort(lo, mid)
            yield from sort(mid + 1, hi)
            yield from merge(lo, hi, 1)

    return tuple(sort(0, n - 1))


_SORT_NETWORK = _oddeven_merge_sort(SORT_N)

INT_MIN = -2 ** 31


def _rms(x, g):
    return x * lax.rsqrt(jnp.mean(x * x, axis=-1, keepdims=True) + EPS) * g


def _params(*sem):
    return pltpu.CompilerParams(dimension_semantics=sem, vmem_limit_bytes=VMEM_LIMIT)


def _dot_nt(a, b):
    return lax.dot_general(a, b, (((1,), (1,)), ((), ())), preferred_element_type=F32)


def _ffn_kernel(x_ref, g_ref, wa_ref, wb_ref, wo_ref, fg_ref, o_ref, xn_ref, acc_ref, *, n_ff, final):
    j = pl.program_id(1)

    @pl.when(j == 0)
    def _():
        xn_ref[...] = _rms(x_ref[...], g_ref[...]).astype(BF16)
        acc_ref[...] = jnp.zeros_like(acc_ref)

    xn = xn_ref[...]
    a = jnp.dot(xn, wa_ref[...], preferred_element_type=F32)
    b = jnp.dot(xn, wb_ref[...], preferred_element_type=F32)
    act = (a * jax.nn.sigmoid(a) * b).astype(BF16)
    acc_ref[...] += jnp.dot(act, wo_ref[...], preferred_element_type=F32)

    @pl.when(j == n_ff - 1)
    def _():
        y = x_ref[...] + 0.5 * acc_ref[...]
        if final:
            y = _rms(y, fg_ref[...])
        o_ref[...] = y


def _ffn(h, g, w_in, w_out, fg, layer, *, final):
    T, D = h.shape
    F = w_out.shape[1]
    n_ff = 2 if F % (2 * LANES) == 0 else 1
    fc = F // n_ff
    tm = min(FFN_TM, T)
    return pl.pallas_call(
        functools.partial(_ffn_kernel, n_ff=n_ff, final=final),
        out_shape=jax.ShapeDtypeStruct((T, D), F32),
        grid=(T // tm, n_ff),
        in_specs=[
            pl.BlockSpec((tm, D), lambda i, j: (i, 0)),
            pl.BlockSpec((None, 1, D), lambda i, j: (layer, 0, 0)),
            pl.BlockSpec((None, D, fc), lambda i, j: (layer, 0, j)),
            pl.BlockSpec((None, D, fc), lambda i, j: (layer, 0, n_ff + j)),
            pl.BlockSpec((None, fc, D), lambda i, j: (layer, j, 0)),
            pl.BlockSpec((1, D), lambda i, j: (0, 0)),
        ],
        out_specs=pl.BlockSpec((tm, D), lambda i, j: (i, 0)),
        scratch_shapes=[pltpu.VMEM((tm, D), BF16), pltpu.VMEM((tm, D), F32)],
        compiler_params=_params("parallel", "arbitrary"),
        name="ffn",
    )(h, g, w_in, w_in, w_out, fg)


_PROJ_W = ("wq", "wukt", "wckv", "kvg", "wiqt", "wik", "wiwt", "wz", "wqc")


def _proj_kernel(h_ref, g_ref, wq_ref, wukt_ref, wckv_ref, kvg_ref, wiqt_ref, wik_ref, wiwt_ref, wz_ref, wqc_ref,
                 qlt_ref, ckv_ref, ckvt_ref, iqt_ref, ik_ref, iwt_ref, z_ref, qc_ref):
    u = _rms(h_ref[...], g_ref[...]).astype(BF16)
    q = jnp.dot(u, wq_ref[...], preferred_element_type=F32).astype(BF16)
    qlt_ref[...] = (_dot_nt(wukt_ref[...], q) * (A_HEAD_DIM ** -0.5 * LOG2E)).astype(BF16)
    c = _rms(jnp.dot(u, wckv_ref[...], preferred_element_type=F32), kvg_ref[...])
    ckv_ref[...] = c.astype(BF16)
    ckvt_ref[...] = c.T.astype(BF16)
    iqt_ref[...] = _dot_nt(wiqt_ref[...], u).astype(BF16)
    ik_ref[...] = jnp.dot(u, wik_ref[...], preferred_element_type=F32)[:, :IDX_DIM].astype(BF16)
    iwt = _dot_nt(wiwt_ref[...], u)[:IDX_HEADS]
    iwt_ref[...] = iwt * ((IDX_DIM ** -0.5) * (IDX_HEADS ** -0.5))
    z_ref[...] = jnp.dot(u, wz_ref[...], preferred_element_type=F32)
    qc = jnp.dot(u, wqc_ref[...], preferred_element_type=F32) * (C_HEAD_DIM ** -0.5)
    qc_ref[...] = qc.astype(BF16)


def _proj(h, g, wp, layer, B, S):
    T, D = h.shape
    tm = min(PROJ_TM, S)
    nt = S // tm
    RL = A_HEADS * KV_RANK
    QI = IDX_HEADS * IDX_DIM

    def wspec(arr):
        return pl.BlockSpec((None,) + arr.shape[1:], lambda b, i: (layer,) + (0,) * (arr.ndim - 1))

    tok = lambda w: pl.BlockSpec((tm, w), lambda b, i: (b * nt + i, 0))
    tokt = lambda w: pl.BlockSpec((None, w, tm), lambda b, i: (b, 0, i))
    return pl.pallas_call(
        _proj_kernel,
        out_shape=[
            jax.ShapeDtypeStruct((B, RL, S), BF16),
            jax.ShapeDtypeStruct((T, KV_RANK), BF16),
            jax.ShapeDtypeStruct((B, KV_RANK, S), BF16),
            jax.ShapeDtypeStruct((B, QI, S), BF16),
            jax.ShapeDtypeStruct((T, IDX_DIM), BF16),
            jax.ShapeDtypeStruct((B, IDX_HEADS, S), F32),
            jax.ShapeDtypeStruct((T, POOL_WIDTH), F32),
            jax.ShapeDtypeStruct((T, C_WIDTH), BF16),
        ],
        grid=(B, nt),
        in_specs=[tok(D), wspec(g)] + [wspec(wp[k]) for k in _PROJ_W],
        out_specs=[tokt(RL), tok(KV_RANK), tokt(KV_RANK), tokt(QI), tok(IDX_DIM), tokt(IDX_HEADS),
                   tok(POOL_WIDTH), tok(C_WIDTH)],
        compiler_params=_params("parallel", "parallel"),
        name="proj",
    )(h, g, *[wp[k] for k in _PROJ_W])


def _t5_bucket_np(dist):
    max_exact = REL_BUCKETS // 2
    n = np.maximum(dist, 0)
    nf = np.maximum(n, 1).astype(np.float32)
    large = max_exact + (np.log(nf / np.float32(max_exact)) / np.float32(math.log(REL_MAX_DIST / max_exact))
                         * np.float32(REL_BUCKETS - max_exact)).astype(np.int32)
    large = np.minimum(large, REL_BUCKETS - 1)
    return np.where(n < max_exact, n, large).astype(np.int32)


def _bias_kernel(rb_ref, bucket_ref, o_ref):
    for k in range(2):
        bk = bucket_ref[k]
        for h in range(A_HEADS):
            far = rb_ref[REL_BUCKETS - 1, h]
            acc = jnp.zeros(bk.shape, F32)
            for b in range(REL_BUCKETS - 1):
                acc = jnp.where(bk == b, (rb_ref[b, h] - far) * LOG2E, acc)
            o_ref[k, :, h * Q_BLOCK:(h + 1) * Q_BLOCK] = acc


def _bias_tables(rel_bias):
    c = np.arange(NEAR_W)[:, None]
    r = np.arange(Q_BLOCK)[None, :]
    buckets = np.stack([_t5_bucket_np(r + Q_BLOCK - c), _t5_bucket_np(r - c)])
    return pl.pallas_call(
        _bias_kernel,
        out_shape=jax.ShapeDtypeStruct((2, NEAR_W, A_HEADS * Q_BLOCK), F32),
        in_specs=[pl.BlockSpec(memory_space=pltpu.SMEM), pl.BlockSpec(memory_space=pltpu.VMEM)],
        out_specs=pl.BlockSpec(memory_space=pltpu.VMEM),
        name="bias_tables",
    )(rel_bias, jnp.asarray(buckets))


def _key_to_f32(k):
    return lax.bitcast_convert_type(jnp.where(k < 0, INT_MIN - k, k), F32)


def _f32_to_key(x):
    b = lax.bitcast_convert_type(x, jnp.int32)
    return jnp.where(b < 0, INT_MIN - b, b)


def _dsa_kernel(qlt_ref, iqt_ref, iwt_ref, ik_ref, ckv_ref, ckvt_ref, tbl_ref, wuvt_ref, o_ref,
                sc_ref, lay_ref, lmax_ref, thr_ref, cnt_ref, qs_ref, iqs_ref, m_ref, l_ref, acc_ref, fold_ref,
                *, k_sel, seq):
    tq = Q_BLOCK
    H = A_HEADS
    i = pl.program_id(1)
    row0 = i * tq
    n_idx = (row0 + tq + IDX_TK - 1) // IDX_TK

    eye_b = lax.broadcasted_iota(jnp.int32, (tq, tq), 0) == lax.broadcasted_iota(jnp.int32, (tq, tq), 1)
    for h in range(H):
        qs_ref[0:KV_RANK, h * tq:(h + 1) * tq] = qlt_ref[h * KV_RANK:(h + 1) * KV_RANK, :]
        iqs_ref[:, h * tq:(h + 1) * tq] = iqt_ref[h * IDX_DIM:(h + 1) * IDX_DIM, :]

    w_rows = [iwt_ref[h:h + 1, :] for h in range(IDX_HEADS)]
    qid = row0 + lax.broadcasted_iota(jnp.int32, (IDX_TK, tq), 1)
    kid0 = lax.broadcasted_iota(jnp.int32, (IDX_TK, tq), 0)
    fold_ref[...] = jnp.full(fold_ref.shape, -jnp.inf, F32)
    lmax_ref[...] = jnp.full(lmax_ref.shape, -jnp.inf, F32)

    def idx_tile(j, on_diagonal):
        r0 = pl.multiple_of(j * IDX_TK, IDX_TK)
        d = jnp.dot(ik_ref[pl.ds(r0, IDX_TK), :], iqs_ref[...], preferred_element_type=F32)
        s = jnp.maximum(d[:, 0:tq], 0.0) * w_rows[0]
        for h in range(1, IDX_HEADS):
            s = s + jnp.maximum(d[:, h * tq:(h + 1) * tq], 0.0) * w_rows[h]
        if on_diagonal:
            s = jnp.where(kid0 + r0 <= qid, s, -jnp.inf)
        sc_ref[pl.ds(r0, IDX_TK), :] = s
        f = s[0:k_sel]
        for q in range(1, IDX_TK // k_sel):
            f = jnp.maximum(f, s[q * k_sel:(q + 1) * k_sel])
        fold_ref[...] = jnp.maximum(fold_ref[...], f)
        n_grp = IDX_TK // (SORT_N * SUBLANES)
        x = [jnp.concatenate([s[(g * SORT_N + r) * SUBLANES:(g * SORT_N + r + 1) * SUBLANES]
                              for g in range(n_grp)], axis=0) for r in range(SORT_N)]
        for a, b in _SORT_NETWORK:
            x[a], x[b] = jnp.maximum(x[a], x[b]), jnp.minimum(x[a], x[b])
        for r in range(SORT_N):
            lay_ref[pl.ds(r0 + r * LAYER_ROWS, LAYER_ROWS), :] = x[r]
            m = x[r][0:SUBLANES]
            for g in range(1, n_grp):
                m = jnp.maximum(m, x[r][g * SUBLANES:(g + 1) * SUBLANES])
            lmax_ref[r * SUBLANES:(r + 1) * SUBLANES, :] = jnp.maximum(lmax_ref[r * SUBLANES:(r + 1) * SUBLANES, :], m)

    def idx_pair(j2, carry):
        idx_tile(2 * j2, False)
        idx_tile(2 * j2 + 1, False)
        return carry

    def idx_single(j, carry):
        idx_tile(j, False)
        return carry

    n_below = n_idx - 1
    lax.fori_loop(0, n_below // 2, idx_pair, 0)
    lax.fori_loop(2 * (n_below // 2), n_below, idx_single, 0)
    idx_tile(n_below, True)

    n_par = COUNT_PARTIALS

    def count(pred):
        def body(j, acc):
            r0 = pl.multiple_of(j * IDX_TK, IDX_TK)
            ind = jnp.where(pred(sc_ref[pl.ds(r0, IDX_TK), :], r0), 1.0, 0.0)
            return acc + jnp.sum(ind.reshape(n_par, IDX_TK // n_par, tq), axis=0)
        acc = lax.fori_loop(0, n_idx, body, jnp.zeros((IDX_TK // n_par, tq), F32))
        return jnp.sum(acc, axis=0, keepdims=True)

    kf = float(k_sel)
    fold = fold_ref[...]
    lo0 = _f32_to_key(jnp.min(fold, axis=0, keepdims=True))
    hi0 = _f32_to_key(jnp.max(fold, axis=0, keepdims=True))

    def n_open(lo, hi):
        return jnp.sum((lo < hi).astype(F32))

    def run_bisection(count_fn, st):
        for value_mid, steps in ((True, VALUE_MID_STEPS - LAYER_FULL_STEPS), (False, BISECT_FIXED_STEPS)):
            step = functools.partial(bis_step, count_fn, value_mid)
            st = lax.fori_loop(0, steps, lambda _, s2: step(*s2), st)

        def body(ws):
            s2 = lax.fori_loop(0, BISECT_STEPS_PER_TEST, lambda _, s3: step(*s3), ws[:3])
            return s2 + (n_open(s2[0], s2[1]),)

        return lax.while_loop(lambda ws: ws[3] > 0.0, body, st + (n_open(st[0], st[1]),))[:3]

    def bis_step(count_fn, value_mid, lo, hi, c_lo):
        d = hi - lo
        mid = lo + lax.shift_right_logical(d, 1) + (d & 1)
        if value_mid:
            lo_f = jnp.maximum(_key_to_f32(lo), -F32_BIG)
            hi_f = jnp.minimum(_key_to_f32(hi), F32_BIG)
            wide = (hi_f - lo_f) > 0.5 * jnp.maximum(jnp.abs(lo_f), jnp.abs(hi_f))
            vmid = jnp.minimum(jnp.maximum(_f32_to_key(0.5 * lo_f + 0.5 * hi_f), lo + 1), hi)
            mid = jnp.where(wide, vmid, mid)
        at_zero = (lo == 0) & (hi >= KEY_MIN_NORMAL)
        mid = jnp.where((lo < 0) & (hi >= 0), 0, jnp.where(at_zero, KEY_MIN_NORMAL, mid))
        thr = _key_to_f32(mid)
        cnt = count_fn(lambda s, r0: s >= thr)
        active = lo < hi
        ge = cnt >= kf
        lo_n = jnp.where(active & ge, mid, lo)
        hi_lt = jnp.where(at_zero, 0, mid - 1)
        hi_n = jnp.where(active, jnp.where(ge, jnp.where(cnt == kf, mid, hi), hi_lt), hi)
        return lo_n, hi_n, jnp.where(active & ge, cnt, c_lo)

    st0 = (lo0, hi0, jnp.full((1, tq), jnp.inf, F32))
    lo, hi, c_lo = lax.fori_loop(0, LAYER_FULL_STEPS, lambda _, st: bis_step(count, True, *st), st0)
    layer_max = [jnp.max(lmax_ref[r * SUBLANES:(r + 1) * SUBLANES, :], axis=0, keepdims=True)
                 for r in range(SORT_N)]
    lo_f = _key_to_f32(lo)
    n_alive = sum(jnp.max(jnp.where(layer_max[r] >= lo_f, 1.0, 0.0)) for r in range(SORT_N))
    n_quarters = (n_alive.astype(jnp.int32) + SORT_N // 4 - 1) // (SORT_N // 4)

    for k in range(1, 5):
        rows = k * (SORT_N // 4) * LAYER_ROWS

        def count_top(pred, rows=rows):
            def body(j, acc):
                r0 = pl.multiple_of(j * IDX_TK, IDX_TK)
                ind = jnp.where(pred(lay_ref[pl.ds(r0, rows), :], r0), 1.0, 0.0)
                return acc + jnp.sum(ind.reshape(n_par, rows // n_par, tq), axis=0)
            acc = lax.fori_loop(0, n_idx, body, jnp.zeros((rows // n_par, tq), F32))
            return jnp.sum(acc, axis=0, keepdims=True)

        cond = (n_quarters <= 1) if k == 1 else ((n_quarters >= 4) if k == 4 else (n_quarters == k))

        @pl.when(cond)
        def _(count_top=count_top):
            thr_ref[...], _, cnt_ref[...] = run_bisection(count_top, (lo, hi, c_lo))

    thr = _key_to_f32(thr_ref[...])

    n_tie = jnp.sum((cnt_ref[...] > kf).astype(F32))

    @pl.when(n_tie > 0.0)
    def _():
        need = kf - count(lambda s, r0: s > thr)

        def pos_body(_, st):
            plo, phi = st
            mid = (plo + phi) // 2
            c = count(lambda s, r0: (s == thr) & (kid0 + r0 <= mid))
            ok = c >= need
            return jnp.where(ok, plo, mid + 1), jnp.where(ok, mid, phi)

        plo0 = jnp.zeros((1, tq), jnp.int32)
        phi0 = jnp.full((1, tq), seq - 1, jnp.int32)
        last, _ = lax.fori_loop(0, max(1, (seq - 1).bit_length()), pos_body, (plo0, phi0))

        def drop_body(j, carry):
            r0 = pl.multiple_of(j * IDX_TK, IDX_TK)
            s = sc_ref[pl.ds(r0, IDX_TK), :]
            sc_ref[pl.ds(r0, IDX_TK), :] = jnp.where((s == thr) & (kid0 + r0 > last), -jnp.inf, s)
            return carry

        lax.fori_loop(0, n_idx, drop_body, 0)

    def set_shift_rows(neg_shift):
        for h in range(H):
            blk = jnp.where(eye_b, neg_shift[:, h * tq:(h + 1) * tq], 0.0)
            qs_ref[KV_RANK:, h * tq:(h + 1) * tq] = blk.astype(BF16)

    def qk(ckv_t, keep):
        lhs = jnp.concatenate([ckv_t, jnp.where(keep, 1.0, MASK_BIG).astype(BF16)], axis=1)
        return jnp.dot(lhs, qs_ref[...], preferred_element_type=F32)

    near0 = jnp.maximum(row0 - tq, 0)
    n_far = (near0 + ATT_TK - 1) // ATT_TK
    fkid0 = lax.broadcasted_iota(jnp.int32, (ATT_TK, tq), 0)
    n0 = pl.multiple_of(near0, tq)
    first = (i == 0).astype(jnp.int32)
    nkid = n0 + lax.broadcasted_iota(jnp.int32, (NEAR_W, tq), 0)
    nqid = row0 + lax.broadcasted_iota(jnp.int32, (NEAR_W, tq), 1)

    def far_tile(j):
        r0 = pl.multiple_of(j * ATT_TK, ATT_TK)
        keep = (sc_ref[pl.ds(r0, ATT_TK), :] >= thr) & (fkid0 + r0 < near0)
        return ckv_ref[pl.ds(r0, ATT_TK), :], ckvt_ref[:, pl.ds(r0, ATT_TK)], keep

    def near_tile():
        keep = (sc_ref[pl.ds(n0, NEAR_W), :] >= thr) & (nkid <= nqid)
        return ckv_ref[pl.ds(n0, NEAR_W), :], ckvt_ref[:, pl.ds(n0, NEAR_W)], keep

    own = ckvt_ref[:, pl.ds(pl.multiple_of(row0, tq), tq)].astype(F32)
    shift = jnp.concatenate(
        [jnp.sum(qs_ref[0:KV_RANK, h * tq:(h + 1) * tq].astype(F32) * own, axis=0, keepdims=True)
         for h in range(H)], axis=1)
    shift = jnp.clip(shift, 1.0, SHIFT_MAX).astype(BF16).astype(F32)
    set_shift_rows(-shift)

    ckv_n, ckvt_n, keep_n = near_tile()
    p = jnp.exp2(qk(ckv_n, keep_n) + tbl_ref[first])
    l_near = jnp.sum(p, axis=0, keepdims=True)
    acc_ref[...] = jnp.dot(ckvt_n, p.astype(BF16), preferred_element_type=F32)

    def far_group(first, unroll, l_f):
        tiles = [far_tile(first + u) for u in range(unroll)]
        ps = [jnp.exp2(qk(ckv_t, keep)) for ckv_t, _, keep in tiles]
        pv = jnp.dot(tiles[0][1], ps[0].astype(BF16), preferred_element_type=F32)
        for u in range(1, unroll):
            pv = pv + jnp.dot(tiles[u][1], ps[u].astype(BF16), preferred_element_type=F32)
        acc_ref[...] += pv
        for p in ps:
            l_f = l_f + jnp.sum(p, axis=0, keepdims=True)
        return l_f

    l_f = l_near
    done = 0
    for unroll in FAR_UNROLLS:
        n_grp = (n_far - done) // unroll
        l_f = lax.fori_loop(0, n_grp, lambda g, l, done=done, unroll=unroll:
                            far_group(done + g * unroll, unroll, l), l_f)
        done = done + n_grp * unroll

    l_tot = l_f
    acc_ref[...] = acc_ref[...] * (1.0 / l_tot)
    in_range = (l_tot > L_TOT_MIN) & (l_tot < L_TOT_MAX)
    n_bad = jnp.sum(jnp.where(in_range, 0.0, 1.0))

    @pl.when(n_bad > 0.0)
    def _():
        set_shift_rows(jnp.full((1, H * tq), -1.0, F32))
        m_ref[...] = jnp.full(m_ref.shape, M_INIT, F32)
        l_ref[...] = jnp.zeros(l_ref.shape, F32)
        acc_ref[...] = jnp.zeros(acc_ref.shape, F32)

        def attend(ckv_t, ckvt_t, keep, bias):
            lg = qk(ckv_t, keep)
            if bias is not None:
                lg = lg + bias
            m_prev = m_ref[...]
            m_new = jnp.maximum(m_prev, jnp.max(lg, axis=0, keepdims=True))
            alpha = jnp.exp2(m_prev - m_new)
            p = jnp.exp2(lg - m_new)
            l_ref[...] = alpha * l_ref[...] + jnp.sum(p, axis=0, keepdims=True)
            m_ref[...] = m_new
            pv = jnp.dot(ckvt_t, p.astype(BF16), preferred_element_type=F32)
            acc_ref[...] = alpha * acc_ref[...] + pv

        def slow_body(j, carry):
            attend(*far_tile(j), None)
            return carry

        lax.fori_loop(0, n_far, slow_body, 0)
        attend(*near_tile(), tbl_ref[first])
        acc_ref[...] = acc_ref[...] * (1.0 / l_ref[...])

    ot = acc_ref[...].astype(BF16)
    o_st = jnp.concatenate([ot[:, h * tq:(h + 1) * tq] for h in range(H)], axis=0)
    yt = jnp.dot(wuvt_ref[...], o_st, preferred_element_type=F32)
    o_ref[...] = yt.T.astype(BF16)


def _dsa(qlt, iqt, iwt, ik, ckv, ckvt, tbl, wuvt, layer, B, S):
    tq = Q_BLOCK
    nb = S // tq
    RL = A_HEADS * KV_RANK
    QI = IDX_HEADS * IDX_DIM
    k_sel = min(TOPK_MAX, S // 4)
    assert IDX_TK % k_sel == 0 and k_sel % SUBLANES == 0
    return pl.pallas_call(
        functools.partial(_dsa_kernel, k_sel=k_sel, seq=S),
        out_shape=jax.ShapeDtypeStruct((B * S, A_WIDTH), BF16),
        grid=(B, nb),
        in_specs=[
            pl.BlockSpec((None, RL, tq), lambda b, i: (b, 0, i)),
            pl.BlockSpec((None, QI, tq), lambda b, i: (b, 0, i)),
            pl.BlockSpec((None, IDX_HEADS, tq), lambda b, i: (b, 0, i)),
            pl.BlockSpec((None, S, IDX_DIM), lambda b, i: (b, 0, 0)),
            pl.BlockSpec((None, S, KV_RANK), lambda b, i: (b, 0, 0)),
            pl.BlockSpec((None, KV_RANK, S), lambda b, i: (b, 0, 0)),
            pl.BlockSpec((2, NEAR_W, A_HEADS * tq), lambda b, i: (0, 0, 0)),
            pl.BlockSpec((None, A_WIDTH, RL), lambda b, i: (layer, 0, 0)),
        ],
        out_specs=pl.BlockSpec((tq, A_WIDTH), lambda b, i: (b * nb + i, 0)),
        scratch_shapes=[
            pltpu.VMEM((S, tq), F32),
            pltpu.VMEM((S, tq), F32),
            pltpu.VMEM((SORT_N * SUBLANES, tq), F32),
            pltpu.VMEM((1, tq), jnp.int32),
            pltpu.VMEM((1, tq), F32),
            pltpu.VMEM((KV_RANK + tq, A_HEADS * tq), BF16),
            pltpu.VMEM((IDX_DIM, IDX_HEADS * tq), BF16),
            pltpu.VMEM((1, A_HEADS * tq), F32),
            pltpu.VMEM((1, A_HEADS * tq), F32),
            pltpu.VMEM((KV_RANK, A_HEADS * tq), F32),
            pltpu.VMEM((k_sel, tq), F32),
        ],
        compiler_params=_params("parallel", "arbitrary"),
        name="dsa",
    )(qlt, iqt, iwt, ik, ckv, ckvt, tbl, wuvt)


def _memkv_kernel(mem_ref, g_ref, w_ref, kbd_ref, vbd_ref):
    M = mem_ref.shape[0]
    mn = _rms(mem_ref[...], g_ref[...]).astype(BF16)
    kv = jnp.dot(mn, w_ref[...], preferred_element_type=F32)
    kt = kv[:, :C_WIDTH].T
    v = kv[:, C_WIDTH:]
    r = lax.broadcasted_iota(jnp.int32, (C_WIDTH, C_HEADS * M), 0) // C_HEAD_DIM
    c = lax.broadcasted_iota(jnp.int32, (C_WIDTH, C_HEADS * M), 1) // M
    kbd_ref[...] = jnp.where(r == c, jnp.concatenate([kt] * C_HEADS, axis=1), 0.0).astype(BF16)
    r = lax.broadcasted_iota(jnp.int32, (C_HEADS * M, C_WIDTH), 0) // M
    c = lax.broadcasted_iota(jnp.int32, (C_HEADS * M, C_WIDTH), 1) // C_HEAD_DIM
    vbd_ref[...] = jnp.where(r == c, jnp.concatenate([v] * C_HEADS, axis=0), 0.0).astype(BF16)


def _memkv(mem, g, w, layer):
    B, M, D = mem.shape
    return pl.pallas_call(
        _memkv_kernel,
        out_shape=[jax.ShapeDtypeStruct((B, C_WIDTH, C_HEADS * M), BF16),
                   jax.ShapeDtypeStruct((B, C_HEADS * M, C_WIDTH), BF16)],
        grid=(B,),
        in_specs=[pl.BlockSpec((None, M, D), lambda b: (b, 0, 0)),
                  pl.BlockSpec((None, 1, D), lambda b: (layer, 0, 0)),
                  pl.BlockSpec((None, D, 2 * C_WIDTH), lambda b: (layer, 0, 0))],
        out_specs=[pl.BlockSpec((None, C_WIDTH, C_HEADS * M), lambda b: (b, 0, 0)),
                   pl.BlockSpec((None, C_HEADS * M, C_WIDTH), lambda b: (b, 0, 0))],
        compiler_params=_params("parallel"),
        name="memkv",
    )(mem, g, w)


def _mix_kernel(h_ref, g_ref, wg_ref, ya_ref, z_ref, zh_ref, qc_ref, kbd_ref, vbd_ref, wpool_ref, ps_ref,
                wa_ref, wb_ref, wc_ref, wo_ref, o_ref, zx_ref):
    tm, D = h_ref.shape
    i = pl.program_id(1)
    M = kbd_ref.shape[1] // C_HEADS
    h = h_ref[...]
    u = _rms(h, g_ref[...]).astype(BF16)

    z = z_ref[...]
    zx_ref[0:POOL_HALO, :] = jnp.where(i == 0, 0.0, zh_ref[...])
    zx_ref[POOL_HALO:, :] = z
    pos = i * tm + lax.broadcasted_iota(jnp.int32, (tm, 1), 0) + 1
    grp = lax.broadcasted_iota(jnp.int32, (1, POOL_WIDTH), 1) // POOL_GROUP
    win = z
    pooled = jnp.zeros_like(z)
    k = 1
    for gi, w in enumerate(POOL_WINDOWS):
        while k < w:
            win = win + zx_ref[POOL_HALO - k:POOL_HALO - k + tm, :]
            k += 1
        cnt = jnp.minimum(pos, w).astype(F32)
        pooled = jnp.where(grp == gi, win / cnt - z, pooled)
    yb = jnp.dot(pooled.astype(BF16), wpool_ref[...], preferred_element_type=F32) * ps_ref[...]

    lg = jnp.dot(qc_ref[...], kbd_ref[...], preferred_element_type=F32)
    ps = []
    for hc in range(C_HEADS):
        seg = lg[:, hc * M:(hc + 1) * M]
        e = jnp.exp(seg - jnp.max(seg, axis=-1, keepdims=True))
        ps.append(e / jnp.sum(e, axis=-1, keepdims=True))
    p = jnp.concatenate(ps, axis=1).astype(BF16)
    yc = jnp.dot(p, vbd_ref[...], preferred_element_type=F32)

    def gate(n):
        pre = jnp.dot(u, wg_ref[:, n * D:(n + 1) * D], preferred_element_type=F32)
        return jax.nn.sigmoid(pre)

    merged = gate(0) * jnp.dot(ya_ref[...], wa_ref[...], preferred_element_type=F32)
    merged += gate(1) * jnp.dot(yb.astype(BF16), wb_ref[...], preferred_element_type=F32)
    merged += gate(2) * jnp.dot(yc.astype(BF16), wc_ref[...], preferred_element_type=F32)
    o_ref[...] = h + jnp.dot(merged.astype(BF16), wo_ref[...], preferred_element_type=F32)


def _mix(h, g, ya, z, qc, kbd, vbd, wp, layer, B, S):
    T, D = h.shape
    tm = min(MIX_TM, S)
    nt = S // tm
    hb = tm // POOL_HALO

    def wspec(arr):
        return pl.BlockSpec((None,) + arr.shape[1:], lambda b, i: (layer,) + (0,) * (arr.ndim - 1))

    tok = lambda w: pl.BlockSpec((tm, w), lambda b, i: (b * nt + i, 0))
    names = ("wpool", "ps", "wa", "wb", "wc", "wo")
    return pl.pallas_call(
        _mix_kernel,
        out_shape=jax.ShapeDtypeStruct((T, D), F32),
        grid=(B, nt),
        in_specs=[tok(D), wspec(g), wspec(wp["wg"]), tok(A_WIDTH), tok(POOL_WIDTH),
                  pl.BlockSpec((POOL_HALO, POOL_WIDTH), lambda b, i: (jnp.maximum((b * nt + i) * hb - 1, 0), 0)),
                  tok(C_WIDTH),
                  pl.BlockSpec((None,) + kbd.shape[1:], lambda b, i: (b, 0, 0)),
                  pl.BlockSpec((None,) + vbd.shape[1:], lambda b, i: (b, 0, 0))]
                 + [wspec(wp[k]) for k in names],
        out_specs=tok(D),
        scratch_shapes=[pltpu.VMEM((tm + POOL_HALO, POOL_WIDTH), F32)],
        compiler_params=_params("parallel", "parallel"),
        name="mix",
    )(h, g, wp["wg"], ya, z, z, qc, kbd, vbd, *[wp[k] for k in names])


def _block_diag(blocks):
    L, G, r, c = blocks.shape
    eye = jnp.eye(G, dtype=blocks.dtype)
    return jnp.einsum("lgrc,gh->lgrhc", blocks, eye).reshape(L, G * r, G * c)


def _prep_weights(w_in, kv_norm, w_uk, w_uv, w_pool, pool_scale, w_branch_a, w_branch_b, w_branch_c, w_out):
    L, D, _ = w_in.shape
    sizes = (A_WIDTH, KV_RANK, IDX_HEADS * IDX_DIM, IDX_DIM, IDX_HEADS, POOL_WIDTH, C_WIDTH, N_BRANCH * D)
    offs = np.concatenate([[0], np.cumsum(sizes)])
    wq, wckv, wiq, wik, wiw, wz, wqc, wg = [w_in[:, :, offs[n]:offs[n + 1]] for n in range(len(sizes))]
    proj = dict(
        wq=wq.astype(BF16),
        wukt=_block_diag(jnp.transpose(w_uk, (0, 2, 1, 3))).astype(BF16),
        wckv=wckv.astype(BF16),
        kvg=kv_norm[:, None, :],
        wiqt=jnp.transpose(wiq, (0, 2, 1)).astype(BF16),
        wik=jnp.pad(wik, ((0, 0), (0, 0), (0, LANES - IDX_DIM))).astype(BF16),
        wiwt=jnp.pad(jnp.transpose(wiw, (0, 2, 1)), ((0, 0), (0, BF16_ROWS - IDX_HEADS), (0, 0))).astype(BF16),
        wz=wz.astype(BF16),
        wqc=wqc.astype(BF16),
    )
    wuvt = _block_diag(jnp.transpose(w_uv, (0, 2, 3, 1))).astype(BF16)
    mix = dict(
        wg=wg.astype(BF16),
        wpool=_block_diag(w_pool).astype(BF16),
        ps=pool_scale[:, None, :],
        wa=w_branch_a.astype(BF16), wb=w_branch_b.astype(BF16), wc=w_branch_c.astype(BF16),
        wo=w_out.astype(BF16),
    )
    return proj, wuvt, mix


def kernel(x, mem, rel_bias, ffn1_norm, ffn1_w_in, ffn1_w_out, mix_norm, w_in, kv_norm, w_uk, w_uv, w_pool,
           pool_scale, mem_norm, w_mem_kv, w_branch_a, w_branch_b, w_branch_c, w_out, ffn2_norm, ffn2_w_in,
           ffn2_w_out, final_norm):
    B, S, D = x.shape
    L = w_in.shape[0]
    assert S % Q_BLOCK == 0 and S % IDX_TK == 0 and S % ATT_TK == 0 and S >= NEAR_W

    proj_w, wuvt, mix_w = _prep_weights(w_in, kv_norm, w_uk, w_uv, w_pool, pool_scale,
                                        w_branch_a, w_branch_b, w_branch_c, w_out)
    f1_in, f1_out = ffn1_w_in.astype(BF16), ffn1_w_out.astype(BF16)
    f2_in, f2_out = ffn2_w_in.astype(BF16), ffn2_w_out.astype(BF16)
    f1_g, f2_g, mx_g, mm_g = (a[:, None, :] for a in (ffn1_norm, ffn2_norm, mix_norm, mem_norm))
    fg = final_norm[None, :]
    wmem = w_mem_kv.astype(BF16)
    tbl = _bias_tables(rel_bias)

    h = x.reshape(B * S, D)
    for l in range(L):
        h = _ffn(h, f1_g, f1_in, f1_out, fg, l, final=False)
        qlt, ckv, ckvt, iqt, ik, iwt, z, qc = _proj(h, mx_g, proj_w, l, B, S)
        ya = _dsa(qlt, iqt, iwt, ik.reshape(B, S, IDX_DIM), ckv.reshape(B, S, KV_RANK), ckvt, tbl, wuvt, l, B, S)
        kbd, vbd = _memkv(mem, mm_g, wmem, l)
        h = _mix(h, mx_g, ya, z, qc, kbd, vbd, mix_w, l, B, S)
        h = _ffn(h, f2_g, f2_in, f2_out, fg, l, final=(l == L - 1))
    return h.reshape(B, S, D)
```

```python
import functools
import math

import numpy as np
import jax
import jax.numpy as jnp
from jax import lax
from jax.experimental import pallas as pl
from jax.experimental.pallas import tpu as pltpu

F32 = jnp.float32
BF16 = jnp.bfloat16

A_HEADS = 8
A_HEAD_DIM = 64
A_WIDTH = A_HEADS * A_HEAD_DIM
KV_RANK = 128
IDX_HEADS = 8
IDX_DIM = 32
TOPK_MAX = 256
Q_BLOCK = 128
POOL_WINDOWS = (2, 4, 8, 16)
POOL_GROUP = 64
POOL_WIDTH = len(POOL_WINDOWS) * POOL_GROUP
POOL_HALO = max(POOL_WINDOWS)
C_HEADS = 4
C_HEAD_DIM = 64
C_WIDTH = C_HEADS * C_HEAD_DIM
N_BRANCH = 3
REL_BUCKETS = 32
REL_MAX_DIST = 128
EPS = 1e-6
NEG = -1e30
M_INIT = -1e29
LOG2E = math.log2(math.e)
MASK_BIG = -NEG
SHIFT_MAX = 2.0 ** 20
L_TOT_MIN = 2.0 ** -60
L_TOT_MAX = 2.0 ** 100

LANES = 128
SUBLANES = 8
BF16_ROWS = 16
VMEM_LIMIT = 52 * 1024 * 1024
FFN_TM = 512
PROJ_TM = 512
MIX_TM = 512
IDX_TK = 512
ATT_TK = 512
FAR_UNROLLS = (4, 2, 1)
IDX_UNROLLS = (4, 2, 1)
NEAR_W = 2 * Q_BLOCK
COUNT_PARTIALS = 8
SORT_N = 16
LAYER_ROWS = IDX_TK // SORT_N
LAYER_FULL_STEPS = 5
VALUE_MID_STEPS = 12
BISECT_FIXED_STEPS = 5
KEY_MIN_NORMAL = 0x00800000
F32_BIG = 3.0e38
BISECT_STEPS_PER_TEST = 4


def _oddeven_merge_sort(n):
    def merge(lo, hi, r):
        step = r * 2
        if step < hi - lo:
            yield from merge(lo, hi, step)
            yield from merge(lo + r, hi, step)
            yield from ((i, i + r) for i in range(lo + r, hi - r, step))
        else:
            yield (lo, lo + r)

    def sort(lo, hi):
        if hi - lo >= 1:
            mid = lo + (hi - lo) // 2
            yield from sort(lo, mid)
            yield from sort(mid + 1, hi)
            yield from merge(lo, hi, 1)

    return tuple(sort(0, n - 1))


_SORT_NETWORK = _oddeven_merge_sort(SORT_N)

INT_MIN = -2 ** 31


def _rms(x, g):
    return x * lax.rsqrt(jnp.mean(x * x, axis=-1, keepdims=True) + EPS) * g


def _params(*sem):
    return pltpu.CompilerParams(dimension_semantics=sem, vmem_limit_bytes=VMEM_LIMIT)


def _dot_nt(a, b):
    return lax.dot_general(a, b, (((1,), (1,)), ((), ())), preferred_element_type=F32)


def _ffn_kernel(x_ref, g_ref, wa_ref, wb_ref, wo_ref, fg_ref, o_ref, xn_ref, acc_ref, *, n_ff, final):
    j = pl.program_id(1)

    @pl.when(j == 0)
    def _():
        xn_ref[...] = _rms(x_ref[...], g_ref[...]).astype(BF16)
        acc_ref[...] = jnp.zeros_like(acc_ref)

    xn = xn_ref[...]
    a = jnp.dot(xn, wa_ref[...], preferred_element_type=F32)
    b = jnp.dot(xn, wb_ref[...], preferred_element_type=F32)
    act = (a * jax.nn.sigmoid(a) * b).astype(BF16)
    acc_ref[...] += jnp.dot(act, wo_ref[...], preferred_element_type=F32)

    @pl.when(j == n_ff - 1)
    def _():
        y = x_ref[...] + 0.5 * acc_ref[...]
        if final:
            y = _rms(y, fg_ref[...])
        o_ref[...] = y


def _ffn(h, g, w_in, w_out, fg, layer, *, final):
    T, D = h.shape
    F = w_out.shape[1]
    n_ff = 2 if F % (2 * LANES) == 0 else 1
    fc = F // n_ff
    tm = min(FFN_TM, T)
    return pl.pallas_call(
        functools.partial(_ffn_kernel, n_ff=n_ff, final=final),
        out_shape=jax.ShapeDtypeStruct((T, D), F32),
        grid=(T // tm, n_ff),
        in_specs=[
            pl.BlockSpec((tm, D), lambda i, j: (i, 0)),
            pl.BlockSpec((None, 1, D), lambda i, j: (layer, 0, 0)),
            pl.BlockSpec((None, D, fc), lambda i, j: (layer, 0, j)),
            pl.BlockSpec((None, D, fc), lambda i, j: (layer, 0, n_ff + j)),
            pl.BlockSpec((None, fc, D), lambda i, j: (layer, j, 0)),
            pl.BlockSpec((1, D), lambda i, j: (0, 0)),
        ],
        out_specs=pl.BlockSpec((tm, D), lambda i, j: (i, 0)),
        scratch_shapes=[pltpu.VMEM((tm, D), BF16), pltpu.VMEM((tm, D), F32)],
        compiler_params=_params("parallel", "arbitrary"),
        name="ffn",
    )(h, g, w_in, w_in, w_out, fg)


_PROJ_W = ("wq", "wukt", "wckv", "kvg", "wiqt", "wik", "wiwt", "wz", "wqc")


def _proj_kernel(h_ref, g_ref, wq_ref, wukt_ref, wckv_ref, kvg_ref, wiqt_ref, wik_ref, wiwt_ref, wz_ref, wqc_ref,
                 qlt_ref, ckv_ref, ckvt_ref, iqt_ref, ik_ref, iwt_ref, z_ref, qc_ref):
    u = _rms(h_ref[...], g_ref[...]).astype(BF16)
    q = jnp.dot(u, wq_ref[...], preferred_element_type=F32).astype(BF16)
    qlt_ref[...] = (_dot_nt(wukt_ref[...], q) * (A_HEAD_DIM ** -0.5 * LOG2E)).astype(BF16)
    c = _rms(jnp.dot(u, wckv_ref[...], preferred_element_type=F32), kvg_ref[...])
    ckv_ref[...] = c.astype(BF16)
    ckvt_ref[...] = c.T.astype(BF16)
    iqt_ref[...] = _dot_nt(wiqt_ref[...], u).astype(BF16)
    ik_ref[...] = jnp.dot(u, wik_ref[...], preferred_element_type=F32)[:, :IDX_DIM].astype(BF16)
    iwt = _dot_nt(wiwt_ref[...], u)[:IDX_HEADS]
    iwt_ref[...] = iwt * ((IDX_DIM ** -0.5) * (IDX_HEADS ** -0.5))
    z_ref[...] = jnp.dot(u, wz_ref[...], preferred_element_type=F32)
    qc = jnp.dot(u, wqc_ref[...], preferred_element_type=F32) * (C_HEAD_DIM ** -0.5)
    qc_ref[...] = qc.astype(BF16)


def _proj(h, g, wp, layer, B, S):
    T, D = h.shape
    tm = min(PROJ_TM, S)
    nt = S // tm
    RL = A_HEADS * KV_RANK
    QI = IDX_HEADS * IDX_DIM

    def wspec(arr):
        return pl.BlockSpec((None,) + arr.shape[1:], lambda b, i: (layer,) + (0,) * (arr.ndim - 1))

    tok = lambda w: pl.BlockSpec((tm, w), lambda b, i: (b * nt + i, 0))
    tokt = lambda w: pl.BlockSpec((None, w, tm), lambda b, i: (b, 0, i))
    return pl.pallas_call(
        _proj_kernel,
        out_shape=[
            jax.ShapeDtypeStruct((B, RL, S), BF16),
            jax.ShapeDtypeStruct((T, KV_RANK), BF16),
            jax.ShapeDtypeStruct((B, KV_RANK, S), BF16),
            jax.ShapeDtypeStruct((B, QI, S), BF16),
            jax.ShapeDtypeStruct((T, IDX_DIM), BF16),
            jax.ShapeDtypeStruct((B, IDX_HEADS, S), F32),
            jax.ShapeDtypeStruct((T, POOL_WIDTH), F32),
            jax.ShapeDtypeStruct((T, C_WIDTH), BF16),
        ],
        grid=(B, nt),
        in_specs=[tok(D), wspec(g)] + [wspec(wp[k]) for k in _PROJ_W],
        out_specs=[tokt(RL), tok(KV_RANK), tokt(KV_RANK), tokt(QI), tok(IDX_DIM), tokt(IDX_HEADS),
                   tok(POOL_WIDTH), tok(C_WIDTH)],
        compiler_params=_params("parallel", "parallel"),
        name="proj",
    )(h, g, *[wp[k] for k in _PROJ_W])


def _t5_bucket_np(dist):
    max_exact = REL_BUCKETS // 2
    n = np.maximum(dist, 0)
    nf = np.maximum(n, 1).astype(np.float32)
    large = max_exact + (np.log(nf / np.float32(max_exact)) / np.float32(math.log(REL_MAX_DIST / max_exact))
                         * np.float32(REL_BUCKETS - max_exact)).astype(np.int32)
    large = np.minimum(large, REL_BUCKETS - 1)
    return np.where(n < max_exact, n, large).astype(np.int32)


def _bias_kernel(rb_ref, bucket_ref, o_ref):
    for k in range(2):
        bk = bucket_ref[k]
        for h in range(A_HEADS):
            far = rb_ref[REL_BUCKETS - 1, h]
            acc = jnp.zeros(bk.shape, F32)
            for b in range(REL_BUCKETS - 1):
                acc = jnp.where(bk == b, (rb_ref[b, h] - far) * LOG2E, acc)
            o_ref[k, :, h * Q_BLOCK:(h + 1) * Q_BLOCK] = acc


def _bias_tables(rel_bias):
    c = np.arange(NEAR_W)[:, None]
    r = np.arange(Q_BLOCK)[None, :]
    buckets = np.stack([_t5_bucket_np(r + Q_BLOCK - c), _t5_bucket_np(r - c)])
    return pl.pallas_call(
        _bias_kernel,
        out_shape=jax.ShapeDtypeStruct((2, NEAR_W, A_HEADS * Q_BLOCK), F32),
        in_specs=[pl.BlockSpec(memory_space=pltpu.SMEM), pl.BlockSpec(memory_space=pltpu.VMEM)],
        out_specs=pl.BlockSpec(memory_space=pltpu.VMEM),
        name="bias_tables",
    )(rel_bias, jnp.asarray(buckets))


def _key_to_f32(k):
    return lax.bitcast_convert_type(jnp.where(k < 0, INT_MIN - k, k), F32)


def _f32_to_key(x):
    b = lax.bitcast_convert_type(x, jnp.int32)
    return jnp.where(b < 0, INT_MIN - b, b)


def _dsa_kernel(qlt_ref, iqt_ref, iwt_ref, ik_ref, ckv_ref, ckvt_ref, tbl_ref, wuvt_ref, o_ref,
                sc_ref, lay_ref, lmax_ref, thr_ref, cnt_ref, qs_ref, iqs_ref, m_ref, l_ref, acc_ref, fold_ref,
                *, k_sel, seq):
    tq = Q_BLOCK
    H = A_HEADS
    i = pl.program_id(1)
    row0 = i * tq
    n_idx = (row0 + tq + IDX_TK - 1) // IDX_TK

    eye_b = lax.broadcasted_iota(jnp.int32, (tq, tq), 0) == lax.broadcasted_iota(jnp.int32, (tq, tq), 1)
    for h in range(H):
        qs_ref[0:KV_RANK, h * tq:(h + 1) * tq] = qlt_ref[h * KV_RANK:(h + 1) * KV_RANK, :]
        iqs_ref[:, h * tq:(h + 1) * tq] = iqt_ref[h * IDX_DIM:(h + 1) * IDX_DIM, :]

    w_rows = [iwt_ref[h:h + 1, :] for h in range(IDX_HEADS)]
    qid = row0 + lax.broadcasted_iota(jnp.int32, (IDX_TK, tq), 1)
    kid0 = lax.broadcasted_iota(jnp.int32, (IDX_TK, tq), 0)
    fold_ref[...] = jnp.full(fold_ref.shape, -jnp.inf, F32)
    lmax_ref[...] = jnp.full(lmax_ref.shape, -jnp.inf, F32)

    def idx_tile(j, on_diagonal):
        r0 = pl.multiple_of(j * IDX_TK, IDX_TK)
        d = jnp.dot(ik_ref[pl.ds(r0, IDX_TK), :], iqs_ref[...], preferred_element_type=F32)
        s = jnp.maximum(d[:, 0:tq], 0.0) * w_rows[0]
        for h in range(1, IDX_HEADS):
            s = s + jnp.maximum(d[:, h * tq:(h + 1) * tq], 0.0) * w_rows[h]
        if on_diagonal:
            s = jnp.where(kid0 + r0 <= qid, s, -jnp.inf)
        sc_ref[pl.ds(r0, IDX_TK), :] = s
        f = s[0:k_sel]
        for q in range(1, IDX_TK // k_sel):
            f = jnp.maximum(f, s[q * k_sel:(q + 1) * k_sel])
        fold_ref[...] = jnp.maximum(fold_ref[...], f)
        n_grp = IDX_TK // (SORT_N * SUBLANES)
        x = [jnp.concatenate([s[(g * SORT_N + r) * SUBLANES:(g * SORT_N + r + 1) * SUBLANES]
                              for g in range(n_grp)], axis=0) for r in range(SORT_N)]
        for a, b in _SORT_NETWORK:
            x[a], x[b] = jnp.maximum(x[a], x[b]), jnp.minimum(x[a], x[b])
        for r in range(SORT_N):
            lay_ref[pl.ds(r0 + r * LAYER_ROWS, LAYER_ROWS), :] = x[r]
            m = x[r][0:SUBLANES]
            for g in range(1, n_grp):
                m = jnp.maximum(m, x[r][g * SUBLANES:(g + 1) * SUBLANES])
            lmax_ref[r * SUBLANES:(r + 1) * SUBLANES, :] = jnp.maximum(lmax_ref[r * SUBLANES:(r + 1) * SUBLANES, :], m)

    n_below = n_idx - 1
    done = 0
    for unroll in IDX_UNROLLS:
        n_grp = (n_below - done) // unroll

        def idx_group(g, carry, done=done, unroll=unroll):
            for u in range(unroll):
                idx_tile(done + g * unroll + u, False)
            return carry

        lax.fori_loop(0, n_grp, idx_group, 0)
        done = done + n_grp * unroll
    idx_tile(n_below, True)

    n_par = COUNT_PARTIALS

    def count_rows(ref, rows, pred):
        def body(j, acc):
            r0 = pl.multiple_of(j * IDX_TK, IDX_TK)
            ind = jnp.where(pred(ref[pl.ds(r0, rows), :], r0), 1.0, 0.0)
            return acc + jnp.sum(ind.reshape(n_par, rows // n_par, tq), axis=0)

        acc = lax.fori_loop(0, n_idx // 2, lambda j2, a: body(2 * j2 + 1, body(2 * j2, a)),
                            jnp.zeros((rows // n_par, tq), F32))
        acc = lax.fori_loop(2 * (n_idx // 2), n_idx, body, acc)
        return jnp.sum(acc, axis=0, keepdims=True)

    count = functools.partial(count_rows, sc_ref, IDX_TK)

    kf = float(k_sel)
    fold = fold_ref[...]
    lo0 = _f32_to_key(jnp.min(fold, axis=0, keepdims=True))
    hi0 = _f32_to_key(jnp.max(fold, axis=0, keepdims=True))

    def n_open(lo, hi):
        return jnp.sum((lo < hi).astype(F32))

    def run_bisection(count_fn, st):
        for value_mid, steps in ((True, VALUE_MID_STEPS - LAYER_FULL_STEPS), (False, BISECT_FIXED_STEPS)):
            step = functools.partial(bis_step, count_fn, value_mid)
            st = lax.fori_loop(0, steps, lambda _, s2: step(*s2), st)

        def body(ws):
            s2 = lax.fori_loop(0, BISECT_STEPS_PER_TEST, lambda _, s3: step(*s3), ws[:3])
            return s2 + (n_open(s2[0], s2[1]),)

        return lax.while_loop(lambda ws: ws[3] > 0.0, body, st + (n_open(st[0], st[1]),))[:3]

    def bis_step(count_fn, value_mid, lo, hi, c_lo):
        d = hi - lo
        mid = lo + lax.shift_right_logical(d, 1) + (d & 1)
        if value_mid:
            lo_f = jnp.maximum(_key_to_f32(lo), -F32_BIG)
            hi_f = jnp.minimum(_key_to_f32(hi), F32_BIG)
            wide = (hi_f - lo_f) > 0.5 * jnp.maximum(jnp.abs(lo_f), jnp.abs(hi_f))
            vmid = jnp.minimum(jnp.maximum(_f32_to_key(0.5 * lo_f + 0.5 * hi_f), lo + 1), hi)
            mid = jnp.where(wide, vmid, mid)
        at_zero = (lo == 0) & (hi >= KEY_MIN_NORMAL)
        mid = jnp.where((lo < 0) & (hi >= 0), 0, jnp.where(at_zero, KEY_MIN_NORMAL, mid))
        thr = _key_to_f32(mid)
        cnt = count_fn(lambda s, r0: s >= thr)
        active = lo < hi
        ge = cnt >= kf
        lo_n = jnp.where(active & ge, mid, lo)
        hi_lt = jnp.where(at_zero, 0, mid - 1)
        hi_n = jnp.where(active, jnp.where(ge, jnp.where(cnt == kf, mid, hi), hi_lt), hi)
        return lo_n, hi_n, jnp.where(active & ge, cnt, c_lo)

    st0 = (lo0, hi0, jnp.full((1, tq), jnp.inf, F32))
    lo, hi, c_lo = lax.fori_loop(0, LAYER_FULL_STEPS, lambda _, st: bis_step(count, True, *st), st0)
    layer_max = [jnp.max(lmax_ref[r * SUBLANES:(r + 1) * SUBLANES, :], axis=0, keepdims=True)
                 for r in range(SORT_N)]
    lo_f = _key_to_f32(lo)
    n_alive = sum(jnp.max(jnp.where(layer_max[r] >= lo_f, 1.0, 0.0)) for r in range(SORT_N))
    n_quarters = (n_alive.astype(jnp.int32) + SORT_N // 4 - 1) // (SORT_N // 4)

    for k in range(1, 5):
        count_top = functools.partial(count_rows, lay_ref, k * (SORT_N // 4) * LAYER_ROWS)

        cond = (n_quarters <= 1) if k == 1 else ((n_quarters >= 4) if k == 4 else (n_quarters == k))

        @pl.when(cond)
        def _(count_top=count_top):
            thr_ref[...], _, cnt_ref[...] = run_bisection(count_top, (lo, hi, c_lo))

    thr = _key_to_f32(thr_ref[...])

    n_tie = jnp.sum((cnt_ref[...] > kf).astype(F32))

    @pl.when(n_tie > 0.0)
    def _():
        need = kf - count(lambda s, r0: s > thr)

        def pos_body(_, st):
            plo, phi = st
            mid = (plo + phi) // 2
            c = count(lambda s, r0: (s == thr) & (kid0 + r0 <= mid))
            ok = c >= need
            return jnp.where(ok, plo, mid + 1), jnp.where(ok, mid, phi)

        plo0 = jnp.zeros((1, tq), jnp.int32)
        phi0 = jnp.full((1, tq), seq - 1, jnp.int32)
        last, _ = lax.fori_loop(0, max(1, (seq - 1).bit_length()), pos_body, (plo0, phi0))

        def drop_body(j, carry):
            r0 = pl.multiple_of(j * IDX_TK, IDX_TK)
            s = sc_ref[pl.ds(r0, IDX_TK), :]
            sc_ref[pl.ds(r0, IDX_TK), :] = jnp.where((s == thr) & (kid0 + r0 > last), -jnp.inf, s)
            return carry

        lax.fori_loop(0, n_idx, drop_body, 0)

    def set_shift_rows(neg_shift):
        for h in range(H):
            blk = jnp.where(eye_b, neg_shift[:, h * tq:(h + 1) * tq], 0.0)
            qs_ref[KV_RANK:, h * tq:(h + 1) * tq] = blk.astype(BF16)

    def qk(ckv_t, keep):
        lhs = jnp.concatenate([ckv_t, jnp.where(keep, 1.0, MASK_BIG).astype(BF16)], axis=1)
        return jnp.dot(lhs, qs_ref[...], preferred_element_type=F32)

    near0 = jnp.maximum(row0 - tq, 0)
    n_far = (near0 + ATT_TK - 1) // ATT_TK
    fkid0 = lax.broadcasted_iota(jnp.int32, (ATT_TK, tq), 0)
    n0 = pl.multiple_of(near0, tq)
    first = (i == 0).astype(jnp.int32)
    nkid = n0 + lax.broadcasted_iota(jnp.int32, (NEAR_W, tq), 0)
    nqid = row0 + lax.broadcasted_iota(jnp.int32, (NEAR_W, tq), 1)

    def far_tile(j):
        r0 = pl.multiple_of(j * ATT_TK, ATT_TK)
        keep = (sc_ref[pl.ds(r0, ATT_TK), :] >= thr) & (fkid0 + r0 < near0)
        return ckv_ref[pl.ds(r0, ATT_TK), :], ckvt_ref[:, pl.ds(r0, ATT_TK)], keep

    def near_tile():
        keep = (sc_ref[pl.ds(n0, NEAR_W), :] >= thr) & (nkid <= nqid)
        return ckv_ref[pl.ds(n0, NEAR_W), :], ckvt_ref[:, pl.ds(n0, NEAR_W)], keep

    own = ckvt_ref[:, pl.ds(pl.multiple_of(row0, tq), tq)].astype(F32)
    shift = jnp.concatenate(
        [jnp.sum(qs_ref[0:KV_RANK, h * tq:(h + 1) * tq].astype(F32) * own, axis=0, keepdims=True)
         for h in range(H)], axis=1)
    shift = jnp.clip(shift, 1.0, SHIFT_MAX).astype(BF16).astype(F32)
    set_shift_rows(-shift)

    ckv_n, ckvt_n, keep_n = near_tile()
    p = jnp.exp2(qk(ckv_n, keep_n) + tbl_ref[first])
    l_near = jnp.sum(p, axis=0, keepdims=True)
    acc_ref[...] = jnp.dot(ckvt_n, p.astype(BF16), preferred_element_type=F32)

    def far_group(first, unroll, l_f):
        tiles = [far_tile(first + u) for u in range(unroll)]
        ps = [jnp.exp2(qk(ckv_t, keep)) for ckv_t, _, keep in tiles]
        pv = jnp.dot(tiles[0][1], ps[0].astype(BF16), preferred_element_type=F32)
        for u in range(1, unroll):
            pv = pv + jnp.dot(tiles[u][1], ps[u].astype(BF16), preferred_element_type=F32)
        acc_ref[...] += pv
        for p in ps:
            l_f = l_f + jnp.sum(p, axis=0, keepdims=True)
        return l_f

    l_f = l_near
    done = 0
    for unroll in FAR_UNROLLS:
        n_grp = (n_far - done) // unroll
        l_f = lax.fori_loop(0, n_grp, lambda g, l, done=done, unroll=unroll:
                            far_group(done + g * unroll, unroll, l), l_f)
        done = done + n_grp * unroll

    l_tot = l_f
    acc_ref[...] = acc_ref[...] * (1.0 / l_tot)
    in_range = (l_tot > L_TOT_MIN) & (l_tot < L_TOT_MAX)
    n_bad = jnp.sum(jnp.where(in_range, 0.0, 1.0))

    @pl.when(n_bad > 0.0)
    def _():
        set_shift_rows(jnp.full((1, H * tq), -1.0, F32))
        m_ref[...] = jnp.full(m_ref.shape, M_INIT, F32)
        l_ref[...] = jnp.zeros(l_ref.shape, F32)
        acc_ref[...] = jnp.zeros(acc_ref.shape, F32)

        def attend(ckv_t, ckvt_t, keep, bias):
            lg = qk(ckv_t, keep)
            if bias is not None:
                lg = lg + bias
            m_prev = m_ref[...]
            m_new = jnp.maximum(m_prev, jnp.max(lg, axis=0, keepdims=True))
            alpha = jnp.exp2(m_prev - m_new)
            p = jnp.exp2(lg - m_new)
            l_ref[...] = alpha * l_ref[...] + jnp.sum(p, axis=0, keepdims=True)
            m_ref[...] = m_new
            pv = jnp.dot(ckvt_t, p.astype(BF16), preferred_element_type=F32)
            acc_ref[...] = alpha * acc_ref[...] + pv

        def slow_body(j, carry):
            attend(*far_tile(j), None)
            return carry

        lax.fori_loop(0, n_far, slow_body, 0)
        attend(*near_tile(), tbl_ref[first])
        acc_ref[...] = acc_ref[...] * (1.0 / l_ref[...])

    ot = acc_ref[...].astype(BF16)
    o_st = jnp.concatenate([ot[:, h * tq:(h + 1) * tq] for h in range(H)], axis=0)
    yt = jnp.dot(wuvt_ref[...], o_st, preferred_element_type=F32)
    o_ref[...] = yt.T.astype(BF16)


def _dsa(qlt, iqt, iwt, ik, ckv, ckvt, tbl, wuvt, layer, B, S):
    tq = Q_BLOCK
    nb = S // tq
    RL = A_HEADS * KV_RANK
    QI = IDX_HEADS * IDX_DIM
    k_sel = min(TOPK_MAX, S // 4)
    assert IDX_TK % k_sel == 0 and k_sel % SUBLANES == 0
    return pl.pallas_call(
        functools.partial(_dsa_kernel, k_sel=k_sel, seq=S),
        out_shape=jax.ShapeDtypeStruct((B * S, A_WIDTH), BF16),
        grid=(B, nb),
        in_specs=[
            pl.BlockSpec((None, RL, tq), lambda b, i: (b, 0, i)),
            pl.BlockSpec((None, QI, tq), lambda b, i: (b, 0, i)),
            pl.BlockSpec((None, IDX_HEADS, tq), lambda b, i: (b, 0, i)),
            pl.BlockSpec((None, S, IDX_DIM), lambda b, i: (b, 0, 0)),
            pl.BlockSpec((None, S, KV_RANK), lambda b, i: (b, 0, 0)),
            pl.BlockSpec((None, KV_RANK, S), lambda b, i: (b, 0, 0)),
            pl.BlockSpec((2, NEAR_W, A_HEADS * tq), lambda b, i: (0, 0, 0)),
            pl.BlockSpec((None, A_WIDTH, RL), lambda b, i: (layer, 0, 0)),
        ],
        out_specs=pl.BlockSpec((tq, A_WIDTH), lambda b, i: (b * nb + i, 0)),
        scratch_shapes=[
            pltpu.VMEM((S, tq), F32),
            pltpu.VMEM((S, tq), F32),
            pltpu.VMEM((SORT_N * SUBLANES, tq), F32),
            pltpu.VMEM((1, tq), jnp.int32),
            pltpu.VMEM((1, tq), F32),
            pltpu.VMEM((KV_RANK + tq, A_HEADS * tq), BF16),
            pltpu.VMEM((IDX_DIM, IDX_HEADS * tq), BF16),
            pltpu.VMEM((1, A_HEADS * tq), F32),
            pltpu.VMEM((1, A_HEADS * tq), F32),
            pltpu.VMEM((KV_RANK, A_HEADS * tq), F32),
            pltpu.VMEM((k_sel, tq), F32),
        ],
        compiler_params=_params("parallel", "arbitrary"),
        name="dsa",
    )(qlt, iqt, iwt, ik, ckv, ckvt, tbl, wuvt)


def _memkv_kernel(mem_ref, g_ref, w_ref, kbd_ref, vbd_ref):
    M = mem_ref.shape[0]
    mn = _rms(mem_ref[...], g_ref[...]).astype(BF16)
    kv = jnp.dot(mn, w_ref[...], preferred_element_type=F32)
    kt = kv[:, :C_WIDTH].T
    v = kv[:, C_WIDTH:]
    r = lax.broadcasted_iota(jnp.int32, (C_WIDTH, C_HEADS * M), 0) // C_HEAD_DIM
    c = lax.broadcasted_iota(jnp.int32, (C_WIDTH, C_HEADS * M), 1) // M
    kbd_ref[...] = jnp.where(r == c, jnp.concatenate([kt] * C_HEADS, axis=1), 0.0).astype(BF16)
    r = lax.broadcasted_iota(jnp.int32, (C_HEADS * M, C_WIDTH), 0) // M
    c = lax.broadcasted_iota(jnp.int32, (C_HEADS * M, C_WIDTH), 1) // C_HEAD_DIM
    vbd_ref[...] = jnp.where(r == c, jnp.concatenate([v] * C_HEADS, axis=0), 0.0).astype(BF16)


def _memkv(mem, g, w, layer):
    B, M, D = mem.shape
    return pl.pallas_call(
        _memkv_kernel,
        out_shape=[jax.ShapeDtypeStruct((B, C_WIDTH, C_HEADS * M), BF16),
                   jax.ShapeDtypeStruct((B, C_HEADS * M, C_WIDTH), BF16)],
        grid=(B,),
        in_specs=[pl.BlockSpec((None, M, D), lambda b: (b, 0, 0)),
                  pl.BlockSpec((None, 1, D), lambda b: (layer, 0, 0)),
                  pl.BlockSpec((None, D, 2 * C_WIDTH), lambda b: (layer, 0, 0))],
        out_specs=[pl.BlockSpec((None, C_WIDTH, C_HEADS * M), lambda b: (b, 0, 0)),
                   pl.BlockSpec((None, C_HEADS * M, C_WIDTH), lambda b: (b, 0, 0))],
        compiler_params=_params("parallel"),
        name="memkv",
    )(mem, g, w)


def _mix_kernel(h_ref, g_ref, wg_ref, ya_ref, z_ref, zh_ref, qc_ref, kbd_ref, vbd_ref, wpool_ref, ps_ref,
                wa_ref, wb_ref, wc_ref, wo_ref, o_ref, zx_ref):
    tm, D = h_ref.shape
    i = pl.program_id(1)
    M = kbd_ref.shape[1] // C_HEADS
    h = h_ref[...]
    u = _rms(h, g_ref[...]).astype(BF16)

    z = z_ref[...]
    zx_ref[0:POOL_HALO, :] = jnp.where(i == 0, 0.0, zh_ref[...])
    zx_ref[POOL_HALO:, :] = z
    pos = i * tm + lax.broadcasted_iota(jnp.int32, (tm, 1), 0) + 1
    grp = lax.broadcasted_iota(jnp.int32, (1, POOL_WIDTH), 1) // POOL_GROUP
    win = z
    pooled = jnp.zeros_like(z)
    k = 1
    for gi, w in enumerate(POOL_WINDOWS):
        while k < w:
            win = win + zx_ref[POOL_HALO - k:POOL_HALO - k + tm, :]
            k += 1
        cnt = jnp.minimum(pos, w).astype(F32)
        pooled = jnp.where(grp == gi, win / cnt - z, pooled)
    yb = jnp.dot(pooled.astype(BF16), wpool_ref[...], preferred_element_type=F32) * ps_ref[...]

    lg = jnp.dot(qc_ref[...], kbd_ref[...], preferred_element_type=F32)
    ps = []
    for hc in range(C_HEADS):
        seg = lg[:, hc * M:(hc + 1) * M]
        e = jnp.exp(seg - jnp.max(seg, axis=-1, keepdims=True))
        ps.append(e / jnp.sum(e, axis=-1, keepdims=True))
    p = jnp.concatenate(ps, axis=1).astype(BF16)
    yc = jnp.dot(p, vbd_ref[...], preferred_element_type=F32)

    def gate(n):
        pre = jnp.dot(u, wg_ref[:, n * D:(n + 1) * D], preferred_element_type=F32)
        return jax.nn.sigmoid(pre)

    merged = gate(0) * jnp.dot(ya_ref[...], wa_ref[...], preferred_element_type=F32)
    merged += gate(1) * jnp.dot(yb.astype(BF16), wb_ref[...], preferred_element_type=F32)
    merged += gate(2) * jnp.dot(yc.astype(BF16), wc_ref[...], preferred_element_type=F32)
    o_ref[...] = h + jnp.dot(merged.astype(BF16), wo_ref[...], preferred_element_type=F32)


def _mix(h, g, ya, z, qc, kbd, vbd, wp, layer, B, S):
    T, D = h.shape
    tm = min(MIX_TM, S)
    nt = S // tm
    hb = tm // POOL_HALO

    def wspec(arr):
        return pl.BlockSpec((None,) + arr.shape[1:], lambda b, i: (layer,) + (0,) * (arr.ndim - 1))

    tok = lambda w: pl.BlockSpec((tm, w), lambda b, i: (b * nt + i, 0))
    names = ("wpool", "ps", "wa", "wb", "wc", "wo")
    return pl.pallas_call(
        _mix_kernel,
        out_shape=jax.ShapeDtypeStruct((T, D), F32),
        grid=(B, nt),
        in_specs=[tok(D), wspec(g), wspec(wp["wg"]), tok(A_WIDTH), tok(POOL_WIDTH),
                  pl.BlockSpec((POOL_HALO, POOL_WIDTH), lambda b, i: (jnp.maximum((b * nt + i) * hb - 1, 0), 0)),
                  tok(C_WIDTH),
                  pl.BlockSpec((None,) + kbd.shape[1:], lambda b, i: (b, 0, 0)),
                  pl.BlockSpec((None,) + vbd.shape[1:], lambda b, i: (b, 0, 0))]
                 + [wspec(wp[k]) for k in names],
        out_specs=tok(D),
        scratch_shapes=[pltpu.VMEM((tm + POOL_HALO, POOL_WIDTH), F32)],
        compiler_params=_params("parallel", "parallel"),
        name="mix",
    )(h, g, wp["wg"], ya, z, z, qc, kbd, vbd, *[wp[k] for k in names])


def _block_diag(blocks):
    L, G, r, c = blocks.shape
    eye = jnp.eye(G, dtype=blocks.dtype)
    return jnp.einsum("lgrc,gh->lgrhc", blocks, eye).reshape(L, G * r, G * c)


def _prep_weights(w_in, kv_norm, w_uk, w_uv, w_pool, pool_scale, w_branch_a, w_branch_b, w_branch_c, w_out):
    L, D, _ = w_in.shape
    sizes = (A_WIDTH, KV_RANK, IDX_HEADS * IDX_DIM, IDX_DIM, IDX_HEADS, POOL_WIDTH, C_WIDTH, N_BRANCH * D)
    offs = np.concatenate([[0], np.cumsum(sizes)])
    wq, wckv, wiq, wik, wiw, wz, wqc, wg = [w_in[:, :, offs[n]:offs[n + 1]] for n in range(len(sizes))]
    proj = dict(
        wq=wq.astype(BF16),
        wukt=_block_diag(jnp.transpose(w_uk, (0, 2, 1, 3))).astype(BF16),
        wckv=wckv.astype(BF16),
        kvg=kv_norm[:, None, :],
        wiqt=jnp.transpose(wiq, (0, 2, 1)).astype(BF16),
        wik=jnp.pad(wik, ((0, 0), (0, 0), (0, LANES - IDX_DIM))).astype(BF16),
        wiwt=jnp.pad(jnp.transpose(wiw, (0, 2, 1)), ((0, 0), (0, BF16_ROWS - IDX_HEADS), (0, 0))).astype(BF16),
        wz=wz.astype(BF16),
        wqc=wqc.astype(BF16),
    )
    wuvt = _block_diag(jnp.transpose(w_uv, (0, 2, 3, 1))).astype(BF16)
    mix = dict(
        wg=wg.astype(BF16),
        wpool=_block_diag(w_pool).astype(BF16),
        ps=pool_scale[:, None, :],
        wa=w_branch_a.astype(BF16), wb=w_branch_b.astype(BF16), wc=w_branch_c.astype(BF16),
        wo=w_out.astype(BF16),
    )
    return proj, wuvt, mix


def kernel(x, mem, rel_bias, ffn1_norm, ffn1_w_in, ffn1_w_out, mix_norm, w_in, kv_norm, w_uk, w_uv, w_pool,
           pool_scale, mem_norm, w_mem_kv, w_branch_a, w_branch_b, w_branch_c, w_out, ffn2_norm, ffn2_w_in,
           ffn2_w_out, final_norm):
    B, S, D = x.shape
    L = w_in.shape[0]
    assert S % Q_BLOCK == 0 and S % IDX_TK == 0 and S % ATT_TK == 0 and S >= NEAR_W

    proj_w, wuvt, mix_w = _prep_weights(w_in, kv_norm, w_uk, w_uv, w_pool, pool_scale,
                                        w_branch_a, w_branch_b, w_branch_c, w_out)
    f1_in, f1_out = ffn1_w_in.astype(BF16), ffn1_w_out.astype(BF16)
    f2_in, f2_out = ffn2_w_in.astype(BF16), ffn2_w_out.astype(BF16)
    f1_g, f2_g, mx_g, mm_g = (a[:, None, :] for a in (ffn1_norm, ffn2_norm, mix_norm, mem_norm))
    fg = final_norm[None, :]
    wmem = w_mem_kv.astype(BF16)
    tbl = _bias_tables(rel_bias)

    h = x.reshape(B * S, D)
    for l in range(L):
        h = _ffn(h, f1_g, f1_in, f1_out, fg, l, final=False)
        qlt, ckv, ckvt, iqt, ik, iwt, z, qc = _proj(h, mx_g, proj_w, l, B, S)
        ya = _dsa(qlt, iqt, iwt, ik.reshape(B, S, IDX_DIM), ckv.reshape(B, S, KV_RANK), ckvt, tbl, wuvt, l, B, S)
        kbd, vbd = _memkv(mem, mm_g, wmem, l)
        h = _mix(h, mx_g, ya, z, qc, kbd, vbd, mix_w, l, B, S)
        h = _ffn(h, f2_g, f2_in, f2_out, fg, l, final=(l == L - 1))
    return h.reshape(B, S, D)
```

```python
import functools
import math

import numpy as np
import jax
import jax.numpy as jnp
from jax import lax
from jax.experimental import pallas as pl
from jax.experimental.pallas import tpu as pltpu

F32 = jnp.float32
BF16 = jnp.bfloat16

A_HEADS = 8
A_HEAD_DIM = 64
A_WIDTH = A_HEADS * A_HEAD_DIM
KV_RANK = 128
IDX_HEADS = 8
IDX_DIM = 32
TOPK_MAX = 256
Q_BLOCK = 128
POOL_WINDOWS = (2, 4, 8, 16)
POOL_GROUP = 64
POOL_WIDTH = len(POOL_WINDOWS) * POOL_GROUP
POOL_HALO = max(POOL_WINDOWS)
C_HEADS = 4
C_HEAD_DIM = 64
C_WIDTH = C_HEADS * C_HEAD_DIM
N_BRANCH = 3
REL_BUCKETS = 32
REL_MAX_DIST = 128
EPS = 1e-6
NEG = -1e30
M_INIT = -1e29
LOG2E = math.log2(math.e)
MASK_BIG = -NEG
SHIFT_MAX = 2.0 ** 20
L_TOT_MIN = 2.0 ** -60
L_TOT_MAX = 2.0 ** 100

LANES = 128
SUBLANES = 8
BF16_ROWS = 16
VMEM_LIMIT = 52 * 1024 * 1024
FFN_TM = 512
PROJ_TM = 512
MIX_TM = 512
IDX_TK = 512
ATT_TK = 512
FAR_UNROLLS = (4, 2, 1)
IDX_UNROLLS = (4, 2, 1)
NEAR_W = 2 * Q_BLOCK
COUNT_PARTIALS = 8
SORT_N = 16
LAYER_ROWS = IDX_TK // SORT_N
LAYER_FULL_STEPS = 5
VALUE_MID_STEPS = 12
BISECT_FIXED_STEPS = 5
KEY_MIN_NORMAL = 0x00800000
F32_BIG = 3.0e38
BISECT_STEPS_PER_TEST = 4


def _oddeven_merge_sort(n):
    def merge(lo, hi, r):
        step = r * 2
        if step < hi - lo:
            yield from merge(lo, hi, step)
            yield from merge(lo + r, hi, step)
            yield from ((i, i + r) for i in range(lo + r, hi - r, step))
        else:
            yield (lo, lo + r)

    def sort(lo, hi):
        if hi - lo >= 1:
            mid = lo + (hi - lo) // 2
            yield from sort(lo, mid)
            yield from sort(mid + 1, hi)
            yield from merge(lo, hi, 1)

    return tuple(sort(0, n - 1))


_SORT_NETWORK = _oddeven_merge_sort(SORT_N)

INT_MIN = -2 ** 31


def _rms(x, g):
    return x * lax.rsqrt(jnp.mean(x * x, axis=-1, keepdims=True) + EPS) * g


def _params(*sem):
    return pltpu.CompilerParams(dimension_semantics=sem, vmem_limit_bytes=VMEM_LIMIT)


def _dot_nt(a, b):
    return lax.dot_general(a, b, (((1,), (1,)), ((), ())), preferred_element_type=F32)


def _ffn_kernel(x_ref, g_ref, wa_ref, wb_ref, wo_ref, fg_ref, o_ref, xn_ref, acc_ref, *, n_ff, final):
    j = pl.program_id(1)

    @pl.when(j == 0)
    def _():
        xn_ref[...] = _rms(x_ref[...], g_ref[...]).astype(BF16)
        acc_ref[...] = jnp.zeros_like(acc_ref)

    xn = xn_ref[...]
    a = jnp.dot(xn, wa_ref[...], preferred_element_type=F32)
    b = jnp.dot(xn, wb_ref[...], preferred_element_type=F32)
    act = (a * jax.nn.sigmoid(a) * b).astype(BF16)
    acc_ref[...] += jnp.dot(act, wo_ref[...], preferred_element_type=F32)

    @pl.when(j == n_ff - 1)
    def _():
        y = x_ref[...] + 0.5 * acc_ref[...]
        if final:
            y = _rms(y, fg_ref[...])
        o_ref[...] = y


def _ffn(h, g, w_in, w_out, fg, layer, *, final):
    T, D = h.shape
    F = w_out.shape[1]
    n_ff = 2 if F % (2 * LANES) == 0 else 1
    fc = F // n_ff
    tm = min(FFN_TM, T)
    return pl.pallas_call(
        functools.partial(_ffn_kernel, n_ff=n_ff, final=final),
        out_shape=jax.ShapeDtypeStruct((T, D), F32),
        grid=(T // tm, n_ff),
        in_specs=[
            pl.BlockSpec((tm, D), lambda i, j: (i, 0)),
            pl.BlockSpec((None, 1, D), lambda i, j: (layer, 0, 0)),
            pl.BlockSpec((None, D, fc), lambda i, j: (layer, 0, j)),
            pl.BlockSpec((None, D, fc), lambda i, j: (layer, 0, n_ff + j)),
            pl.BlockSpec((None, fc, D), lambda i, j: (layer, j, 0)),
            pl.BlockSpec((1, D), lambda i, j: (0, 0)),
        ],
        out_specs=pl.BlockSpec((tm, D), lambda i, j: (i, 0)),
        scratch_shapes=[pltpu.VMEM((tm, D), BF16), pltpu.VMEM((tm, D), F32)],
        compiler_params=_params("parallel", "arbitrary"),
        name="ffn",
    )(h, g, w_in, w_in, w_out, fg)


_PROJ_W = ("wq", "wukt", "wckv", "kvg", "wiqt", "wik", "wiwt", "wz", "wqc")


def _proj_kernel(h_ref, g_ref, wq_ref, wukt_ref, wckv_ref, kvg_ref, wiqt_ref, wik_ref, wiwt_ref, wz_ref, wqc_ref,
                 qlt_ref, ckv_ref, ckvt_ref, iqt_ref, ik_ref, iwt_ref, z_ref, qc_ref):
    u = _rms(h_ref[...], g_ref[...]).astype(BF16)
    q = jnp.dot(u, wq_ref[...], preferred_element_type=F32).astype(BF16)
    qlt_ref[...] = (_dot_nt(wukt_ref[...], q) * (A_HEAD_DIM ** -0.5 * LOG2E)).astype(BF16)
    c = _rms(jnp.dot(u, wckv_ref[...], preferred_element_type=F32), kvg_ref[...])
    ckv_ref[...] = c.astype(BF16)
    ckvt_ref[...] = c.T.astype(BF16)
    iqt_ref[...] = _dot_nt(wiqt_ref[...], u).astype(BF16)
    ik_ref[...] = jnp.dot(u, wik_ref[...], preferred_element_type=F32)[:, :IDX_DIM].astype(BF16)
    iwt = _dot_nt(wiwt_ref[...], u)[:IDX_HEADS]
    iwt_ref[...] = iwt * ((IDX_DIM ** -0.5) * (IDX_HEADS ** -0.5))
    z_ref[...] = jnp.dot(u, wz_ref[...], preferred_element_type=F32)
    qc = jnp.dot(u, wqc_ref[...], preferred_element_type=F32) * (C_HEAD_DIM ** -0.5)
    qc_ref[...] = qc.astype(BF16)


def _proj(h, g, wp, layer, B, S):
    T, D = h.shape
    tm = min(PROJ_TM, S)
    nt = S // tm
    RL = A_HEADS * KV_RANK
    QI = IDX_HEADS * IDX_DIM

    def wspec(arr):
        return pl.BlockSpec((None,) + arr.shape[1:], lambda b, i: (layer,) + (0,) * (arr.ndim - 1))

    tok = lambda w: pl.BlockSpec((tm, w), lambda b, i: (b * nt + i, 0))
    tokt = lambda w: pl.BlockSpec((None, w, tm), lambda b, i: (b, 0, i))
    return pl.pallas_call(
        _proj_kernel,
        out_shape=[
            jax.ShapeDtypeStruct((B, RL, S), BF16),
            jax.ShapeDtypeStruct((T, KV_RANK), BF16),
            jax.ShapeDtypeStruct((B, KV_RANK, S), BF16),
            jax.ShapeDtypeStruct((B, QI, S), BF16),
            jax.ShapeDtypeStruct((T, IDX_DIM), BF16),
            jax.ShapeDtypeStruct((B, IDX_HEADS, S), F32),
            jax.ShapeDtypeStruct((T, POOL_WIDTH), F32),
            jax.ShapeDtypeStruct((T, C_WIDTH), BF16),
        ],
        grid=(B, nt),
        in_specs=[tok(D), wspec(g)] + [wspec(wp[k]) for k in _PROJ_W],
        out_specs=[tokt(RL), tok(KV_RANK), tokt(KV_RANK), tokt(QI), tok(IDX_DIM), tokt(IDX_HEADS),
                   tok(POOL_WIDTH), tok(C_WIDTH)],
        compiler_params=_params("parallel", "parallel"),
        name="proj",
    )(h, g, *[wp[k] for k in _PROJ_W])


def _t5_bucket_np(dist):
    max_exact = REL_BUCKETS // 2
    n = np.maximum(dist, 0)
    nf = np.maximum(n, 1).astype(np.float32)
    large = max_exact + (np.log(nf / np.float32(max_exact)) / np.float32(math.log(REL_MAX_DIST / max_exact))
                         * np.float32(REL_BUCKETS - max_exact)).astype(np.int32)
    large = np.minimum(large, REL_BUCKETS - 1)
    return np.where(n < max_exact, n, large).astype(np.int32)


def _bias_kernel(rb_ref, bucket_ref, o_ref):
    for k in range(2):
        bk = bucket_ref[k]
        for h in range(A_HEADS):
            far = rb_ref[REL_BUCKETS - 1, h]
            acc = jnp.zeros(bk.shape, F32)
            for b in range(REL_BUCKETS - 1):
                acc = jnp.where(bk == b, (rb_ref[b, h] - far) * LOG2E, acc)
            o_ref[k, :, h * Q_BLOCK:(h + 1) * Q_BLOCK] = acc


def _bias_tables(rel_bias):
    c = np.arange(NEAR_W)[:, None]
    r = np.arange(Q_BLOCK)[None, :]
    buckets = np.stack([_t5_bucket_np(r + Q_BLOCK - c), _t5_bucket_np(r - c)])
    return pl.pallas_call(
        _bias_kernel,
        out_shape=jax.ShapeDtypeStruct((2, NEAR_W, A_HEADS * Q_BLOCK), F32),
        in_specs=[pl.BlockSpec(memory_space=pltpu.SMEM), pl.BlockSpec(memory_space=pltpu.VMEM)],
        out_specs=pl.BlockSpec(memory_space=pltpu.VMEM),
        name="bias_tables",
    )(rel_bias, jnp.asarray(buckets))


def _key_to_f32(k):
    return lax.bitcast_convert_type(jnp.where(k < 0, INT_MIN - k, k), F32)


def _f32_to_key(x):
    b = lax.bitcast_convert_type(x, jnp.int32)
    return jnp.where(b < 0, INT_MIN - b, b)


def _dsa_kernel(qlt_ref, iqt_ref, iwt_ref, ik_ref, ckv_ref, ckvt_ref, tbl_ref, wuvt_ref, o_ref,
                sc_ref, lay_ref, lmax_ref, thr_ref, cnt_ref, qs_ref, iqs_ref, m_ref, l_ref, acc_ref, fold_ref,
                *, k_sel, seq):
    tq = Q_BLOCK
    H = A_HEADS
    i = pl.program_id(1)
    row0 = i * tq
    n_idx = (row0 + tq + IDX_TK - 1) // IDX_TK

    eye_b = lax.broadcasted_iota(jnp.int32, (tq, tq), 0) == lax.broadcasted_iota(jnp.int32, (tq, tq), 1)
    for h in range(H):
        qs_ref[0:KV_RANK, h * tq:(h + 1) * tq] = qlt_ref[h * KV_RANK:(h + 1) * KV_RANK, :]
        iqs_ref[:, h * tq:(h + 1) * tq] = iqt_ref[h * IDX_DIM:(h + 1) * IDX_DIM, :]

    w_rows = [iwt_ref[h:h + 1, :] for h in range(IDX_HEADS)]
    qid = row0 + lax.broadcasted_iota(jnp.int32, (IDX_TK, tq), 1)
    kid0 = lax.broadcasted_iota(jnp.int32, (IDX_TK, tq), 0)
    fold_ref[...] = jnp.full(fold_ref.shape, -jnp.inf, F32)
    lmax_ref[...] = jnp.full(lmax_ref.shape, -jnp.inf, F32)

    def idx_tile(j, on_diagonal):
        r0 = pl.multiple_of(j * IDX_TK, IDX_TK)
        d = jnp.dot(ik_ref[pl.ds(r0, IDX_TK), :], iqs_ref[...], preferred_element_type=F32)
        s = jnp.maximum(d[:, 0:tq], 0.0) * w_rows[0]
        for h in range(1, IDX_HEADS):
            s = s + jnp.maximum(d[:, h * tq:(h + 1) * tq], 0.0) * w_rows[h]
        if on_diagonal:
            s = jnp.where(kid0 + r0 <= qid, s, -jnp.inf)
        sc_ref[pl.ds(r0, IDX_TK), :] = s
        f = s[0:k_sel]
        for q in range(1, IDX_TK // k_sel):
            f = jnp.maximum(f, s[q * k_sel:(q + 1) * k_sel])
        fold_ref[...] = jnp.maximum(fold_ref[...], f)
        n_grp = IDX_TK // (SORT_N * SUBLANES)
        x = [jnp.concatenate([s[(g * SORT_N + r) * SUBLANES:(g * SORT_N + r + 1) * SUBLANES]
                              for g in range(n_grp)], axis=0) for r in range(SORT_N)]
        for a, b in _SORT_NETWORK:
            x[a], x[b] = jnp.maximum(x[a], x[b]), jnp.minimum(x[a], x[b])
        for r in range(SORT_N):
            lay_ref[pl.ds(r0 + r * LAYER_ROWS, LAYER_ROWS), :] = x[r]
            m = x[r][0:SUBLANES]
            for g in range(1, n_grp):
                m = jnp.maximum(m, x[r][g * SUBLANES:(g + 1) * SUBLANES])
            lmax_ref[r * SUBLANES:(r + 1) * SUBLANES, :] = jnp.maximum(lmax_ref[r * SUBLANES:(r + 1) * SUBLANES, :], m)

    n_below = n_idx - 1
    done = 0
    for unroll in IDX_UNROLLS:
        n_grp = (n_below - done) // unroll

        def idx_group(g, carry, done=done, unroll=unroll):
            for u in range(unroll):
                idx_tile(done + g * unroll + u, False)
            return carry

        lax.fori_loop(0, n_grp, idx_group, 0)
        done = done + n_grp * unroll
    idx_tile(n_below, True)

    n_par = COUNT_PARTIALS

    def count_rows(ref, rows, pred):
        def body(j, acc):
            r0 = pl.multiple_of(j * IDX_TK, IDX_TK)
            ind = jnp.where(pred(ref[pl.ds(r0, rows), :], r0), 1.0, 0.0)
            return acc + jnp.sum(ind.reshape(n_par, rows // n_par, tq), axis=0)

        acc = lax.fori_loop(0, n_idx // 2, lambda j2, a: body(2 * j2 + 1, body(2 * j2, a)),
                            jnp.zeros((rows // n_par, tq), F32))
        acc = lax.fori_loop(2 * (n_idx // 2), n_idx, body, acc)
        return jnp.sum(acc, axis=0, keepdims=True)

    count = functools.partial(count_rows, sc_ref, IDX_TK)

    kf = float(k_sel)
    fold = fold_ref[...]
    lo0 = _f32_to_key(jnp.min(fold, axis=0, keepdims=True))
    hi0 = _f32_to_key(jnp.max(fold, axis=0, keepdims=True))

    def n_open(lo, hi):
        return jnp.sum((lo < hi).astype(F32))

    def run_bisection(count_fn, st):
        for value_mid, steps in ((True, VALUE_MID_STEPS - LAYER_FULL_STEPS), (False, BISECT_FIXED_STEPS)):
            step = functools.partial(bis_step, count_fn, value_mid)
            st = lax.fori_loop(0, steps, lambda _, s2: step(*s2), st)

        def body(ws):
            s2 = lax.fori_loop(0, BISECT_STEPS_PER_TEST, lambda _, s3: step(*s3), ws[:3])
            return s2 + (n_open(s2[0], s2[1]),)

        return lax.while_loop(lambda ws: ws[3] > 0.0, body, st + (n_open(st[0], st[1]),))[:3]

    def bis_step(count_fn, value_mid, lo, hi, c_lo):
        d = hi - lo
        mid = lo + lax.shift_right_logical(d, 1) + (d & 1)
        if value_mid:
            lo_f = jnp.maximum(_key_to_f32(lo), -F32_BIG)
            hi_f = jnp.minimum(_key_to_f32(hi), F32_BIG)
            wide = (hi_f - lo_f) > 0.5 * jnp.maximum(jnp.abs(lo_f), jnp.abs(hi_f))
            vmid = jnp.minimum(jnp.maximum(_f32_to_key(0.5 * lo_f + 0.5 * hi_f), lo + 1), hi)
            mid = jnp.where(wide, vmid, mid)
        at_zero = (lo == 0) & (hi >= KEY_MIN_NORMAL)
        mid = jnp.where((lo < 0) & (hi >= 0), 0, jnp.where(at_zero, KEY_MIN_NORMAL, mid))
        thr = _key_to_f32(mid)
        cnt = count_fn(lambda s, r0: s >= thr)
        active = lo < hi
        ge = cnt >= kf
        lo_n = jnp.where(active & ge, mid, lo)
        hi_lt = jnp.where(at_zero, 0, mid - 1)
        hi_n = jnp.where(active, jnp.where(ge, jnp.where(cnt == kf, mid, hi), hi_lt), hi)
        return lo_n, hi_n, jnp.where(active & ge, cnt, c_lo)

    st0 = (lo0, hi0, jnp.full((1, tq), jnp.inf, F32))
    lo, hi, c_lo = lax.fori_loop(0, LAYER_FULL_STEPS, lambda _, st: bis_step(count, True, *st), st0)
    layer_max = [jnp.max(lmax_ref[r * SUBLANES:(r + 1) * SUBLANES, :], axis=0, keepdims=True)
                 for r in range(SORT_N)]
    lo_f = _key_to_f32(lo)
    alive = jnp.concatenate([jnp.where(layer_max[r] >= lo_f, 1.0, 0.0) for r in range(SORT_N)], axis=0)
    n_alive = jnp.sum(jnp.max(alive, axis=1, keepdims=True))
    n_quarters = (n_alive.astype(jnp.int32) + SORT_N // 4 - 1) // (SORT_N // 4)

    for k in range(1, 5):
        count_top = functools.partial(count_rows, lay_ref, k * (SORT_N // 4) * LAYER_ROWS)

        cond = (n_quarters <= 1) if k == 1 else ((n_quarters >= 4) if k == 4 else (n_quarters == k))

        @pl.when(cond)
        def _(count_top=count_top):
            thr_ref[...], _, cnt_ref[...] = run_bisection(count_top, (lo, hi, c_lo))

    thr = _key_to_f32(thr_ref[...])

    n_tie = jnp.sum((cnt_ref[...] > kf).astype(F32))

    @pl.when(n_tie > 0.0)
    def _():
        need = kf - count(lambda s, r0: s > thr)

        def pos_body(_, st):
            plo, phi = st
            mid = (plo + phi) // 2
            c = count(lambda s, r0: (s == thr) & (kid0 + r0 <= mid))
            ok = c >= need
            return jnp.where(ok, plo, mid + 1), jnp.where(ok, mid, phi)

        plo0 = jnp.zeros((1, tq), jnp.int32)
        phi0 = jnp.full((1, tq), seq - 1, jnp.int32)
        last, _ = lax.fori_loop(0, max(1, (seq - 1).bit_length()), pos_body, (plo0, phi0))

        def drop_body(j, carry):
            r0 = pl.multiple_of(j * IDX_TK, IDX_TK)
            s = sc_ref[pl.ds(r0, IDX_TK), :]
            sc_ref[pl.ds(r0, IDX_TK), :] = jnp.where((s == thr) & (kid0 + r0 > last), -jnp.inf, s)
            return carry

        lax.fori_loop(0, n_idx, drop_body, 0)

    def set_shift_rows(neg_shift):
        for h in range(H):
            blk = jnp.where(eye_b, neg_shift[:, h * tq:(h + 1) * tq], 0.0)
            qs_ref[KV_RANK:, h * tq:(h + 1) * tq] = blk.astype(BF16)

    def qk(ckv_t, keep):
        lhs = jnp.concatenate([ckv_t, jnp.where(keep, 1.0, MASK_BIG).astype(BF16)], axis=1)
        return jnp.dot(lhs, qs_ref[...], preferred_element_type=F32)

    near0 = jnp.maximum(row0 - tq, 0)
    n_far = (near0 + ATT_TK - 1) // ATT_TK
    fkid0 = lax.broadcasted_iota(jnp.int32, (ATT_TK, tq), 0)
    n0 = pl.multiple_of(near0, tq)
    first = (i == 0).astype(jnp.int32)
    nkid = n0 + lax.broadcasted_iota(jnp.int32, (NEAR_W, tq), 0)
    nqid = row0 + lax.broadcasted_iota(jnp.int32, (NEAR_W, tq), 1)

    def far_tile(j):
        r0 = pl.multiple_of(j * ATT_TK, ATT_TK)
        keep = (sc_ref[pl.ds(r0, ATT_TK), :] >= thr) & (fkid0 + r0 < near0)
        return ckv_ref[pl.ds(r0, ATT_TK), :], ckvt_ref[:, pl.ds(r0, ATT_TK)], keep

    def near_tile():
        keep = (sc_ref[pl.ds(n0, NEAR_W), :] >= thr) & (nkid <= nqid)
        return ckv_ref[pl.ds(n0, NEAR_W), :], ckvt_ref[:, pl.ds(n0, NEAR_W)], keep

    own = ckvt_ref[:, pl.ds(pl.multiple_of(row0, tq), tq)].astype(F32)
    shift = jnp.concatenate(
        [jnp.sum(qs_ref[0:KV_RANK, h * tq:(h + 1) * tq].astype(F32) * own, axis=0, keepdims=True)
         for h in range(H)], axis=1)
    shift = jnp.clip(shift, 1.0, SHIFT_MAX).astype(BF16).astype(F32)
    set_shift_rows(-shift)

    ckv_n, ckvt_n, keep_n = near_tile()
    p = jnp.exp2(qk(ckv_n, keep_n) + tbl_ref[first])
    l_near = jnp.sum(p, axis=0, keepdims=True)
    acc_ref[...] = jnp.dot(ckvt_n, p.astype(BF16), preferred_element_type=F32)

    def far_group(first, unroll, l_f):
        tiles = [far_tile(first + u) for u in range(unroll)]
        ps = [jnp.exp2(qk(ckv_t, keep)) for ckv_t, _, keep in tiles]
        pv = jnp.dot(tiles[0][1], ps[0].astype(BF16), preferred_element_type=F32)
        for u in range(1, unroll):
            pv = pv + jnp.dot(tiles[u][1], ps[u].astype(BF16), preferred_element_type=F32)
        acc_ref[...] += pv
        for p in ps:
            l_f = l_f + jnp.sum(p, axis=0, keepdims=True)
        return l_f

    l_f = l_near
    done = 0
    for unroll in FAR_UNROLLS:
        n_grp = (n_far - done) // unroll
        l_f = lax.fori_loop(0, n_grp, lambda g, l, done=done, unroll=unroll:
                            far_group(done + g * unroll, unroll, l), l_f)
        done = done + n_grp * unroll

    l_tot = l_f
    acc_ref[...] = acc_ref[...] * (1.0 / l_tot)
    in_range = (l_tot > L_TOT_MIN) & (l_tot < L_TOT_MAX)
    n_bad = jnp.sum(jnp.where(in_range, 0.0, 1.0))

    @pl.when(n_bad > 0.0)
    def _():
        set_shift_rows(jnp.full((1, H * tq), -1.0, F32))
        m_ref[...] = jnp.full(m_ref.shape, M_INIT, F32)
        l_ref[...] = jnp.zeros(l_ref.shape, F32)
        acc_ref[...] = jnp.zeros(acc_ref.shape, F32)

        def attend(ckv_t, ckvt_t, keep, bias):
            lg = qk(ckv_t, keep)
            if bias is not None:
                lg = lg + bias
            m_prev = m_ref[...]
            m_new = jnp.maximum(m_prev, jnp.max(lg, axis=0, keepdims=True))
            alpha = jnp.exp2(m_prev - m_new)
            p = jnp.exp2(lg - m_new)
            l_ref[...] = alpha * l_ref[...] + jnp.sum(p, axis=0, keepdims=True)
            m_ref[...] = m_new
            pv = jnp.dot(ckvt_t, p.astype(BF16), preferred_element_type=F32)
            acc_ref[...] = alpha * acc_ref[...] + pv

        def slow_body(j, carry):
            attend(*far_tile(j), None)
            return carry

        lax.fori_loop(0, n_far, slow_body, 0)
        attend(*near_tile(), tbl_ref[first])
        acc_ref[...] = acc_ref[...] * (1.0 / l_ref[...])

    ot = acc_ref[...].astype(BF16)
    o_st = jnp.concatenate([ot[:, h * tq:(h + 1) * tq] for h in range(H)], axis=0)
    yt = jnp.dot(wuvt_ref[...], o_st, preferred_element_type=F32)
    o_ref[...] = yt.T.astype(BF16)


def _dsa(qlt, iqt, iwt, ik, ckv, ckvt, tbl, wuvt, layer, B, S):
    tq = Q_BLOCK
    nb = S // tq
    RL = A_HEADS * KV_RANK
    QI = IDX_HEADS * IDX_DIM
    k_sel = min(TOPK_MAX, S // 4)
    assert IDX_TK % k_sel == 0 and k_sel % SUBLANES == 0
    return pl.pallas_call(
        functools.partial(_dsa_kernel, k_sel=k_sel, seq=S),
        out_shape=jax.ShapeDtypeStruct((B * S, A_WIDTH), BF16),
        grid=(B, nb),
        in_specs=[
            pl.BlockSpec((None, RL, tq), lambda b, i: (b, 0, i)),
            pl.BlockSpec((None, QI, tq), lambda b, i: (b, 0, i)),
            pl.BlockSpec((None, IDX_HEADS, tq), lambda b, i: (b, 0, i)),
            pl.BlockSpec((None, S, IDX_DIM), lambda b, i: (b, 0, 0)),
            pl.BlockSpec((None, S, KV_RANK), lambda b, i: (b, 0, 0)),
            pl.BlockSpec((None, KV_RANK, S), lambda b, i: (b, 0, 0)),
            pl.BlockSpec((2, NEAR_W, A_HEADS * tq), lambda b, i: (0, 0, 0)),
            pl.BlockSpec((None, A_WIDTH, RL), lambda b, i: (layer, 0, 0)),
        ],
        out_specs=pl.BlockSpec((tq, A_WIDTH), lambda b, i: (b * nb + i, 0)),
        scratch_shapes=[
            pltpu.VMEM((S, tq), F32),
            pltpu.VMEM((S, tq), F32),
            pltpu.VMEM((SORT_N * SUBLANES, tq), F32),
            pltpu.VMEM((1, tq), jnp.int32),
            pltpu.VMEM((1, tq), F32),
            pltpu.VMEM((KV_RANK + tq, A_HEADS * tq), BF16),
            pltpu.VMEM((IDX_DIM, IDX_HEADS * tq), BF16),
            pltpu.VMEM((1, A_HEADS * tq), F32),
            pltpu.VMEM((1, A_HEADS * tq), F32),
            pltpu.VMEM((KV_RANK, A_HEADS * tq), F32),
            pltpu.VMEM((k_sel, tq), F32),
        ],
        compiler_params=_params("parallel", "arbitrary"),
        name="dsa",
    )(qlt, iqt, iwt, ik, ckv, ckvt, tbl, wuvt)


def _memkv_kernel(mem_ref, g_ref, w_ref, kbd_ref, vbd_ref):
    M = mem_ref.shape[0]
    mn = _rms(mem_ref[...], g_ref[...]).astype(BF16)
    kv = jnp.dot(mn, w_ref[...], preferred_element_type=F32)
    kt = kv[:, :C_WIDTH].T
    v = kv[:, C_WIDTH:]
    r = lax.broadcasted_iota(jnp.int32, (C_WIDTH, C_HEADS * M), 0) // C_HEAD_DIM
    c = lax.broadcasted_iota(jnp.int32, (C_WIDTH, C_HEADS * M), 1) // M
    kbd_ref[...] = jnp.where(r == c, jnp.concatenate([kt] * C_HEADS, axis=1), 0.0).astype(BF16)
    r = lax.broadcasted_iota(jnp.int32, (C_HEADS * M, C_WIDTH), 0) // M
    c = lax.broadcasted_iota(jnp.int32, (C_HEADS * M, C_WIDTH), 1) // C_HEAD_DIM
    vbd_ref[...] = jnp.where(r == c, jnp.concatenate([v] * C_HEADS, axis=0), 0.0).astype(BF16)


def _memkv(mem, g, w, layer):
    B, M, D = mem.shape
    return pl.pallas_call(
        _memkv_kernel,
        out_shape=[jax.ShapeDtypeStruct((B, C_WIDTH, C_HEADS * M), BF16),
                   jax.ShapeDtypeStruct((B, C_HEADS * M, C_WIDTH), BF16)],
        grid=(B,),
        in_specs=[pl.BlockSpec((None, M, D), lambda b: (b, 0, 0)),
                  pl.BlockSpec((None, 1, D), lambda b: (layer, 0, 0)),
                  pl.BlockSpec((None, D, 2 * C_WIDTH), lambda b: (layer, 0, 0))],
        out_specs=[pl.BlockSpec((None, C_WIDTH, C_HEADS * M), lambda b: (b, 0, 0)),
                   pl.BlockSpec((None, C_HEADS * M, C_WIDTH), lambda b: (b, 0, 0))],
        compiler_params=_params("parallel"),
        name="memkv",
    )(mem, g, w)


def _mix_kernel(h_ref, g_ref, wg_ref, ya_ref, z_ref, zh_ref, qc_ref, kbd_ref, vbd_ref, wpool_ref, ps_ref,
                wa_ref, wb_ref, wc_ref, wo_ref, o_ref, zx_ref):
    tm, D = h_ref.shape
    i = pl.program_id(1)
    M = kbd_ref.shape[1] // C_HEADS
    h = h_ref[...]
    u = _rms(h, g_ref[...]).astype(BF16)

    z = z_ref[...]
    zx_ref[0:POOL_HALO, :] = jnp.where(i == 0, 0.0, zh_ref[...])
    zx_ref[POOL_HALO:, :] = z
    pos = i * tm + lax.broadcasted_iota(jnp.int32, (tm, 1), 0) + 1
    grp = lax.broadcasted_iota(jnp.int32, (1, POOL_WIDTH), 1) // POOL_GROUP
    win = z
    pooled = jnp.zeros_like(z)
    k = 1
    for gi, w in enumerate(POOL_WINDOWS):
        while k < w:
            win = win + zx_ref[POOL_HALO - k:POOL_HALO - k + tm, :]
            k += 1
        cnt = jnp.minimum(pos, w).astype(F32)
        pooled = jnp.where(grp == gi, win / cnt - z, pooled)
    yb = jnp.dot(pooled.astype(BF16), wpool_ref[...], preferred_element_type=F32) * ps_ref[...]

    lg = jnp.dot(qc_ref[...], kbd_ref[...], preferred_element_type=F32)
    ps = []
    for hc in range(C_HEADS):
        seg = lg[:, hc * M:(hc + 1) * M]
        e = jnp.exp(seg - jnp.max(seg, axis=-1, keepdims=True))
        ps.append(e / jnp.sum(e, axis=-1, keepdims=True))
    p = jnp.concatenate(ps, axis=1).astype(BF16)
    yc = jnp.dot(p, vbd_ref[...], preferred_element_type=F32)

    def gate(n):
        pre = jnp.dot(u, wg_ref[:, n * D:(n + 1) * D], preferred_element_type=F32)
        return jax.nn.sigmoid(pre)

    merged = gate(0) * jnp.dot(ya_ref[...], wa_ref[...], preferred_element_type=F32)
    merged += gate(1) * jnp.dot(yb.astype(BF16), wb_ref[...], preferred_element_type=F32)
    merged += gate(2) * jnp.dot(yc.astype(BF16), wc_ref[...], preferred_element_type=F32)
    o_ref[...] = h + jnp.dot(merged.astype(BF16), wo_ref[...], preferred_element_type=F32)


def _mix(h, g, ya, z, qc, kbd, vbd, wp, layer, B, S):
    T, D = h.shape
    tm = min(MIX_TM, S)
    nt = S // tm
    hb = tm // POOL_HALO

    def wspec(arr):
        return pl.BlockSpec((None,) + arr.shape[1:], lambda b, i: (layer,) + (0,) * (arr.ndim - 1))

    tok = lambda w: pl.BlockSpec((tm, w), lambda b, i: (b * nt + i, 0))
    names = ("wpool", "ps", "wa", "wb", "wc", "wo")
    return pl.pallas_call(
        _mix_kernel,
        out_shape=jax.ShapeDtypeStruct((T, D), F32),
        grid=(B, nt),
        in_specs=[tok(D), wspec(g), wspec(wp["wg"]), tok(A_WIDTH), tok(POOL_WIDTH),
                  pl.BlockSpec((POOL_HALO, POOL_WIDTH), lambda b, i: (jnp.maximum((b * nt + i) * hb - 1, 0), 0)),
                  tok(C_WIDTH),
                  pl.BlockSpec((None,) + kbd.shape[1:], lambda b, i: (b, 0, 0)),
                  pl.BlockSpec((None,) + vbd.shape[1:], lambda b, i: (b, 0, 0))]
                 + [wspec(wp[k]) for k in names],
        out_specs=tok(D),
        scratch_shapes=[pltpu.VMEM((tm + POOL_HALO, POOL_WIDTH), F32)],
        compiler_params=_params("parallel", "parallel"),
        name="mix",
    )(h, g, wp["wg"], ya, z, z, qc, kbd, vbd, *[wp[k] for k in names])


def _block_diag(blocks):
    L, G, r, c = blocks.shape
    eye = jnp.eye(G, dtype=blocks.dtype)
    return jnp.einsum("lgrc,gh->lgrhc", blocks, eye).reshape(L, G * r, G * c)


def _prep_weights(w_in, kv_norm, w_uk, w_uv, w_pool, pool_scale, w_branch_a, w_branch_b, w_branch_c, w_out):
    L, D, _ = w_in.shape
    sizes = (A_WIDTH, KV_RANK, IDX_HEADS * IDX_DIM, IDX_DIM, IDX_HEADS, POOL_WIDTH, C_WIDTH, N_BRANCH * D)
    offs = np.concatenate([[0], np.cumsum(sizes)])
    wq, wckv, wiq, wik, wiw, wz, wqc, wg = [w_in[:, :, offs[n]:offs[n + 1]] for n in range(len(sizes))]
    proj = dict(
        wq=wq.astype(BF16),
        wukt=_block_diag(jnp.transpose(w_uk, (0, 2, 1, 3))).astype(BF16),
        wckv=wckv.astype(BF16),
        kvg=kv_norm[:, None, :],
        wiqt=jnp.transpose(wiq, (0, 2, 1)).astype(BF16),
        wik=jnp.pad(wik, ((0, 0), (0, 0), (0, LANES - IDX_DIM))).astype(BF16),
        wiwt=jnp.pad(jnp.transpose(wiw, (0, 2, 1)), ((0, 0), (0, BF16_ROWS - IDX_HEADS), (0, 0))).astype(BF16),
        wz=wz.astype(BF16),
        wqc=wqc.astype(BF16),
    )
    wuvt = _block_diag(jnp.transpose(w_uv, (0, 2, 3, 1))).astype(BF16)
    mix = dict(
        wg=wg.astype(BF16),
        wpool=_block_diag(w_pool).astype(BF16),
        ps=pool_scale[:, None, :],
        wa=w_branch_a.astype(BF16), wb=w_branch_b.astype(BF16), wc=w_branch_c.astype(BF16),
        wo=w_out.astype(BF16),
    )
    return proj, wuvt, mix


def kernel(x, mem, rel_bias, ffn1_norm, ffn1_w_in, ffn1_w_out, mix_norm, w_in, kv_norm, w_uk, w_uv, w_pool,
           pool_scale, mem_norm, w_mem_kv, w_branch_a, w_branch_b, w_branch_c, w_out, ffn2_norm, ffn2_w_in,
           ffn2_w_out, final_norm):
    B, S, D = x.shape
    L = w_in.shape[0]
    assert S % Q_BLOCK == 0 and S % IDX_TK == 0 and S % ATT_TK == 0 and S >= NEAR_W

    proj_w, wuvt, mix_w = _prep_weights(w_in, kv_norm, w_uk, w_uv, w_pool, pool_scale,
                                        w_branch_a, w_branch_b, w_branch_c, w_out)
    f1_in, f1_out = ffn1_w_in.astype(BF16), ffn1_w_out.astype(BF16)
    f2_in, f2_out = ffn2_w_in.astype(BF16), ffn2_w_out.astype(BF16)
    f1_g, f2_g, mx_g, mm_g = (a[:, None, :] for a in (ffn1_norm, ffn2_norm, mix_norm, mem_norm))
    fg = final_norm[None, :]
    wmem = w_mem_kv.astype(BF16)
    tbl = _bias_tables(rel_bias)

    h = x.reshape(B * S, D)
    for l in range(L):
        h = _ffn(h, f1_g, f1_in, f1_out, fg, l, final=False)
        qlt, ckv, ckvt, iqt, ik, iwt, z, qc = _proj(h, mx_g, proj_w, l, B, S)
        ya = _dsa(qlt, iqt, iwt, ik.reshape(B, S, IDX_DIM), ckv.reshape(B, S, KV_RANK), ckvt, tbl, wuvt, l, B, S)
        kbd, vbd = _memkv(mem, mm_g, wmem, l)
        h = _mix(h, mx_g, ya, z, qc, kbd, vbd, mix_w, l, B, S)
        h = _ffn(h, f2_g, f2_in, f2_out, fg, l, final=(l == L - 1))
    return h.reshape(B, S, D)
```

```python
import functools
import math

import numpy as np
import jax
import jax.numpy as jnp
from jax import lax
from jax.experimental import pallas as pl
from jax.experimental.pallas import tpu as pltpu

F32 = jnp.float32
BF16 = jnp.bfloat16

A_HEADS = 8
A_HEAD_DIM = 64
A_WIDTH = A_HEADS * A_HEAD_DIM
KV_RANK = 128
IDX_HEADS = 8
IDX_DIM = 32
TOPK_MAX = 256
Q_BLOCK = 128
POOL_WINDOWS = (2, 4, 8, 16)
POOL_GROUP = 64
POOL_WIDTH = len(POOL_WINDOWS) * POOL_GROUP
POOL_HALO = max(POOL_WINDOWS)
C_HEADS = 4
C_HEAD_DIM = 64
C_WIDTH = C_HEADS * C_HEAD_DIM
N_BRANCH = 3
REL_BUCKETS = 32
REL_MAX_DIST = 128
EPS = 1e-6
NEG = -1e30
M_INIT = -1e29
LOG2E = math.log2(math.e)
MASK_BIG = -NEG
SHIFT_MAX = 2.0 ** 20
L_TOT_MIN = 2.0 ** -60
L_TOT_MAX = 2.0 ** 100

LANES = 128
SUBLANES = 8
BF16_ROWS = 16
VMEM_LIMIT = 52 * 1024 * 1024
FFN_TM = 512
PROJ_TM = 512
MIX_TM = 512
IDX_TK = 512
ATT_TK = 512
FAR_UNROLLS = (4, 2, 1)
IDX_UNROLLS = (4, 2, 1)
NEAR_W = 2 * Q_BLOCK
COUNT_PARTIALS = 8
COUNT_UNROLL = 4
SORT_N = 16
LAYER_ROWS = IDX_TK // SORT_N
LAYER_FULL_STEPS = 5
VALUE_MID_STEPS = 12
BISECT_FIXED_STEPS = 5
KEY_MIN_NORMAL = 0x00800000
F32_BIG = 3.0e38
BISECT_STEPS_PER_TEST = 4


def _oddeven_merge_sort(n):
    def merge(lo, hi, r):
        step = r * 2
        if step < hi - lo:
            yield from merge(lo, hi, step)
            yield from merge(lo + r, hi, step)
            yield from ((i, i + r) for i in range(lo + r, hi - r, step))
        else:
            yield (lo, lo + r)

    def sort(lo, hi):
        if hi - lo >= 1:
            mid = lo + (hi - lo) // 2
            yield from sort(lo, mid)
            yield from sort(mid + 1, hi)
            yield from merge(lo, hi, 1)

    return tuple(sort(0, n - 1))


_SORT_NETWORK = _oddeven_merge_sort(SORT_N)

INT_MIN = -2 ** 31


def _rms(x, g):
    return x * lax.rsqrt(jnp.mean(x * x, axis=-1, keepdims=True) + EPS) * g


def _params(*sem):
    return pltpu.CompilerParams(dimension_semantics=sem, vmem_limit_bytes=VMEM_LIMIT)


def _dot_nt(a, b):
    return lax.dot_general(a, b, (((1,), (1,)), ((), ())), preferred_element_type=F32)


def _ffn_kernel(x_ref, g_ref, wa_ref, wb_ref, wo_ref, fg_ref, o_ref, xn_ref, acc_ref, *, n_ff, final):
    j = pl.program_id(1)

    @pl.when(j == 0)
    def _():
        xn_ref[...] = _rms(x_ref[...], g_ref[...]).astype(BF16)
        acc_ref[...] = jnp.zeros_like(acc_ref)

    xn = xn_ref[...]
    a = jnp.dot(xn, wa_ref[...], preferred_element_type=F32)
    b = jnp.dot(xn, wb_ref[...], preferred_element_type=F32)
    act = (a * jax.nn.sigmoid(a) * b).astype(BF16)
    acc_ref[...] += jnp.dot(act, wo_ref[...], preferred_element_type=F32)

    @pl.when(j == n_ff - 1)
    def _():
        y = x_ref[...] + 0.5 * acc_ref[...]
        if final:
            y = _rms(y, fg_ref[...])
        o_ref[...] = y


def _ffn(h, g, w_in, w_out, fg, layer, *, final):
    T, D = h.shape
    F = w_out.shape[1]
    n_ff = 2 if F % (2 * LANES) == 0 else 1
    fc = F // n_ff
    tm = min(FFN_TM, T)
    return pl.pallas_call(
        functools.partial(_ffn_kernel, n_ff=n_ff, final=final),
        out_shape=jax.ShapeDtypeStruct((T, D), F32),
        grid=(T // tm, n_ff),
        in_specs=[
            pl.BlockSpec((tm, D), lambda i, j: (i, 0)),
            pl.BlockSpec((None, 1, D), lambda i, j: (layer, 0, 0)),
            pl.BlockSpec((None, D, fc), lambda i, j: (layer, 0, j)),
            pl.BlockSpec((None, D, fc), lambda i, j: (layer, 0, n_ff + j)),
            pl.BlockSpec((None, fc, D), lambda i, j: (layer, j, 0)),
            pl.BlockSpec((1, D), lambda i, j: (0, 0)),
        ],
        out_specs=pl.BlockSpec((tm, D), lambda i, j: (i, 0)),
        scratch_shapes=[pltpu.VMEM((tm, D), BF16), pltpu.VMEM((tm, D), F32)],
        compiler_params=_params("parallel", "arbitrary"),
        name="ffn",
    )(h, g, w_in, w_in, w_out, fg)


_PROJ_W = ("wq", "wukt", "wckv", "kvg", "wiqt", "wik", "wiwt", "wz", "wqc")


def _proj_kernel(h_ref, g_ref, wq_ref, wukt_ref, wckv_ref, kvg_ref, wiqt_ref, wik_ref, wiwt_ref, wz_ref, wqc_ref,
                 qlt_ref, ckv_ref, ckvt_ref, iqt_ref, ik_ref, iwt_ref, z_ref, qc_ref):
    u = _rms(h_ref[...], g_ref[...]).astype(BF16)
    q = jnp.dot(u, wq_ref[...], preferred_element_type=F32).astype(BF16)
    qlt_ref[...] = (_dot_nt(wukt_ref[...], q) * (A_HEAD_DIM ** -0.5 * LOG2E)).astype(BF16)
    c = _rms(jnp.dot(u, wckv_ref[...], preferred_element_type=F32), kvg_ref[...])
    ckv_ref[...] = c.astype(BF16)
    ckvt_ref[...] = c.T.astype(BF16)
    iqt_ref[...] = _dot_nt(wiqt_ref[...], u).astype(BF16)
    ik_ref[...] = jnp.dot(u, wik_ref[...], preferred_element_type=F32)[:, :IDX_DIM].astype(BF16)
    iwt = _dot_nt(wiwt_ref[...], u)[:IDX_HEADS]
    iwt_ref[...] = iwt * ((IDX_DIM ** -0.5) * (IDX_HEADS ** -0.5))
    z_ref[...] = jnp.dot(u, wz_ref[...], preferred_element_type=F32)
    qc = jnp.dot(u, wqc_ref[...], preferred_element_type=F32) * (C_HEAD_DIM ** -0.5)
    qc_ref[...] = qc.astype(BF16)


def _proj(h, g, wp, layer, B, S):
    T, D = h.shape
    tm = min(PROJ_TM, S)
    nt = S // tm
    RL = A_HEADS * KV_RANK
    QI = IDX_HEADS * IDX_DIM

    def wspec(arr):
        return pl.BlockSpec((None,) + arr.shape[1:], lambda b, i: (layer,) + (0,) * (arr.ndim - 1))

    tok = lambda w: pl.BlockSpec((tm, w), lambda b, i: (b * nt + i, 0))
    tokt = lambda w: pl.BlockSpec((None, w, tm), lambda b, i: (b, 0, i))
    return pl.pallas_call(
        _proj_kernel,
        out_shape=[
            jax.ShapeDtypeStruct((B, RL, S), BF16),
            jax.ShapeDtypeStruct((T, KV_RANK), BF16),
            jax.ShapeDtypeStruct((B, KV_RANK, S), BF16),
            jax.ShapeDtypeStruct((B, QI, S), BF16),
            jax.ShapeDtypeStruct((T, IDX_DIM), BF16),
            jax.ShapeDtypeStruct((B, IDX_HEADS, S), F32),
            jax.ShapeDtypeStruct((T, POOL_WIDTH), F32),
            jax.ShapeDtypeStruct((T, C_WIDTH), BF16),
        ],
        grid=(B, nt),
        in_specs=[tok(D), wspec(g)] + [wspec(wp[k]) for k in _PROJ_W],
        out_specs=[tokt(RL), tok(KV_RANK), tokt(KV_RANK), tokt(QI), tok(IDX_DIM), tokt(IDX_HEADS),
                   tok(POOL_WIDTH), tok(C_WIDTH)],
        compiler_params=_params("parallel", "parallel"),
        name="proj",
    )(h, g, *[wp[k] for k in _PROJ_W])


def _t5_bucket_np(dist):
    max_exact = REL_BUCKETS // 2
    n = np.maximum(dist, 0)
    nf = np.maximum(n, 1).astype(np.float32)
    large = max_exact + (np.log(nf / np.float32(max_exact)) / np.float32(math.log(REL_MAX_DIST / max_exact))
                         * np.float32(REL_BUCKETS - max_exact)).astype(np.int32)
    large = np.minimum(large, REL_BUCKETS - 1)
    return np.where(n < max_exact, n, large).astype(np.int32)


def _bias_kernel(rb_ref, bucket_ref, o_ref):
    for k in range(2):
        bk = bucket_ref[k]
        for h in range(A_HEADS):
            far = rb_ref[REL_BUCKETS - 1, h]
            acc = jnp.zeros(bk.shape, F32)
            for b in range(REL_BUCKETS - 1):
                acc = jnp.where(bk == b, (rb_ref[b, h] - far) * LOG2E, acc)
            o_ref[k, :, h * Q_BLOCK:(h + 1) * Q_BLOCK] = acc


def _bias_tables(rel_bias):
    c = np.arange(NEAR_W)[:, None]
    r = np.arange(Q_BLOCK)[None, :]
    buckets = np.stack([_t5_bucket_np(r + Q_BLOCK - c), _t5_bucket_np(r - c)])
    return pl.pallas_call(
        _bias_kernel,
        out_shape=jax.ShapeDtypeStruct((2, NEAR_W, A_HEADS * Q_BLOCK), F32),
        in_specs=[pl.BlockSpec(memory_space=pltpu.SMEM), pl.BlockSpec(memory_space=pltpu.VMEM)],
        out_specs=pl.BlockSpec(memory_space=pltpu.VMEM),
        name="bias_tables",
    )(rel_bias, jnp.asarray(buckets))


def _key_to_f32(k):
    return lax.bitcast_convert_type(jnp.where(k < 0, INT_MIN - k, k), F32)


def _f32_to_key(x):
    b = lax.bitcast_convert_type(x, jnp.int32)
    return jnp.where(b < 0, INT_MIN - b, b)


def _dsa_kernel(qlt_ref, iqt_ref, iwt_ref, ik_ref, ckv_ref, ckvt_ref, tbl_ref, wuvt_ref, o_ref,
                sc_ref, lay_ref, lmax_ref, thr_ref, cnt_ref, qs_ref, iqs_ref, m_ref, l_ref, acc_ref, fold_ref,
                *, k_sel, seq):
    tq = Q_BLOCK
    H = A_HEADS
    i = pl.program_id(1)
    row0 = i * tq
    n_idx = (row0 + tq + IDX_TK - 1) // IDX_TK

    eye_b = lax.broadcasted_iota(jnp.int32, (tq, tq), 0) == lax.broadcasted_iota(jnp.int32, (tq, tq), 1)
    for h in range(H):
        qs_ref[0:KV_RANK, h * tq:(h + 1) * tq] = qlt_ref[h * KV_RANK:(h + 1) * KV_RANK, :]
        iqs_ref[:, h * tq:(h + 1) * tq] = iqt_ref[h * IDX_DIM:(h + 1) * IDX_DIM, :]

    w_rows = [iwt_ref[h:h + 1, :] for h in range(IDX_HEADS)]
    qid = row0 + lax.broadcasted_iota(jnp.int32, (IDX_TK, tq), 1)
    kid0 = lax.broadcasted_iota(jnp.int32, (IDX_TK, tq), 0)
    fold_ref[...] = jnp.full(fold_ref.shape, -jnp.inf, F32)
    lmax_ref[...] = jnp.full(lmax_ref.shape, -jnp.inf, F32)

    def idx_tile(j, on_diagonal):
        r0 = pl.multiple_of(j * IDX_TK, IDX_TK)
        d = jnp.dot(ik_ref[pl.ds(r0, IDX_TK), :], iqs_ref[...], preferred_element_type=F32)
        s = jnp.maximum(d[:, 0:tq], 0.0) * w_rows[0]
        for h in range(1, IDX_HEADS):
            s = s + jnp.maximum(d[:, h * tq:(h + 1) * tq], 0.0) * w_rows[h]
        if on_diagonal:
            s = jnp.where(kid0 + r0 <= qid, s, -jnp.inf)
        sc_ref[pl.ds(r0, IDX_TK), :] = s
        f = s[0:k_sel]
        for q in range(1, IDX_TK // k_sel):
            f = jnp.maximum(f, s[q * k_sel:(q + 1) * k_sel])
        fold_ref[...] = jnp.maximum(fold_ref[...], f)
        n_grp = IDX_TK // (SORT_N * SUBLANES)
        x = [jnp.concatenate([s[(g * SORT_N + r) * SUBLANES:(g * SORT_N + r + 1) * SUBLANES]
                              for g in range(n_grp)], axis=0) for r in range(SORT_N)]
        for a, b in _SORT_NETWORK:
            x[a], x[b] = jnp.maximum(x[a], x[b]), jnp.minimum(x[a], x[b])
        for r in range(SORT_N):
            lay_ref[pl.ds(r0 + r * LAYER_ROWS, LAYER_ROWS), :] = x[r]
            m = x[r][0:SUBLANES]
            for g in range(1, n_grp):
                m = jnp.maximum(m, x[r][g * SUBLANES:(g + 1) * SUBLANES])
            lmax_ref[r * SUBLANES:(r + 1) * SUBLANES, :] = jnp.maximum(lmax_ref[r * SUBLANES:(r + 1) * SUBLANES, :], m)

    n_below = n_idx - 1
    done = 0
    for unroll in IDX_UNROLLS:
        n_grp = (n_below - done) // unroll

        def idx_group(g, carry, done=done, unroll=unroll):
            for u in range(unroll):
                idx_tile(done + g * unroll + u, False)
            return carry

        lax.fori_loop(0, n_grp, idx_group, 0)
        done = done + n_grp * unroll
    idx_tile(n_below, True)

    n_par = COUNT_PARTIALS

    def count_rows(ref, rows, pred):
        def body(j, acc):
            r0 = pl.multiple_of(j * IDX_TK, IDX_TK)
            ind = jnp.where(pred(ref[pl.ds(r0, rows), :], r0), 1.0, 0.0)
            return acc + jnp.sum(ind.reshape(n_par, rows // n_par, tq), axis=0)

        def group(g, a):
            for u in range(COUNT_UNROLL):
                a = body(COUNT_UNROLL * g + u, a)
            return a

        acc = lax.fori_loop(0, n_idx // COUNT_UNROLL, group, jnp.zeros((rows // n_par, tq), F32))
        acc = lax.fori_loop(COUNT_UNROLL * (n_idx // COUNT_UNROLL), n_idx, body, acc)
        return jnp.sum(acc, axis=0, keepdims=True)

    count = functools.partial(count_rows, sc_ref, IDX_TK)

    kf = float(k_sel)
    fold = fold_ref[...]
    lo0 = _f32_to_key(jnp.min(fold, axis=0, keepdims=True))
    hi0 = _f32_to_key(jnp.max(fold, axis=0, keepdims=True))

    def n_open(lo, hi):
        return jnp.sum((lo < hi).astype(F32))

    def run_bisection(count_fn, st):
        for value_mid, steps in ((True, VALUE_MID_STEPS - LAYER_FULL_STEPS), (False, BISECT_FIXED_STEPS)):
            step = functools.partial(bis_step, count_fn, value_mid)
            st = lax.fori_loop(0, steps, lambda _, s2: step(*s2), st)

        def body(ws):
            s2 = lax.fori_loop(0, BISECT_STEPS_PER_TEST, lambda _, s3: step(*s3), ws[:3])
            return s2 + (n_open(s2[0], s2[1]),)

        return lax.while_loop(lambda ws: ws[3] > 0.0, body, st + (n_open(st[0], st[1]),))[:3]

    def bis_step(count_fn, value_mid, lo, hi, c_lo):
        d = hi - lo
        mid = lo + lax.shift_right_logical(d, 1) + (d & 1)
        if value_mid:
            lo_f = jnp.maximum(_key_to_f32(lo), -F32_BIG)
            hi_f = jnp.minimum(_key_to_f32(hi), F32_BIG)
            wide = (hi_f - lo_f) > 0.5 * jnp.maximum(jnp.abs(lo_f), jnp.abs(hi_f))
            vmid = jnp.minimum(jnp.maximum(_f32_to_key(0.5 * lo_f + 0.5 * hi_f), lo + 1), hi)
            mid = jnp.where(wide, vmid, mid)
        at_zero = (lo == 0) & (hi >= KEY_MIN_NORMAL)
        mid = jnp.where((lo < 0) & (hi >= 0), 0, jnp.where(at_zero, KEY_MIN_NORMAL, mid))
        thr = _key_to_f32(mid)
        cnt = count_fn(lambda s, r0: s >= thr)
        active = lo < hi
        ge = cnt >= kf
        lo_n = jnp.where(active & ge, mid, lo)
        hi_lt = jnp.where(at_zero, 0, mid - 1)
        hi_n = jnp.where(active, jnp.where(ge, jnp.where(cnt == kf, mid, hi), hi_lt), hi)
        return lo_n, hi_n, jnp.where(active & ge, cnt, c_lo)

    st0 = (lo0, hi0, jnp.full((1, tq), jnp.inf, F32))
    lo, hi, c_lo = lax.fori_loop(0, LAYER_FULL_STEPS, lambda _, st: bis_step(count, True, *st), st0)
    layer_max = [jnp.max(lmax_ref[r * SUBLANES:(r + 1) * SUBLANES, :], axis=0, keepdims=True)
                 for r in range(SORT_N)]
    lo_f = _key_to_f32(lo)
    n_alive = sum(jnp.max(jnp.where(layer_max[r] >= lo_f, 1.0, 0.0)) for r in range(SORT_N))
    n_quarters = (n_alive.astype(jnp.int32) + SORT_N // 4 - 1) // (SORT_N // 4)

    for k in range(1, 5):
        count_top = functools.partial(count_rows, lay_ref, k * (SORT_N // 4) * LAYER_ROWS)

        cond = (n_quarters <= 1) if k == 1 else ((n_quarters >= 4) if k == 4 else (n_quarters == k))

        @pl.when(cond)
        def _(count_top=count_top):
            thr_ref[...], _, cnt_ref[...] = run_bisection(count_top, (lo, hi, c_lo))

    thr = _key_to_f32(thr_ref[...])

    n_tie = jnp.sum((cnt_ref[...] > kf).astype(F32))

    @pl.when(n_tie > 0.0)
    def _():
        need = kf - count(lambda s, r0: s > thr)

        def pos_body(_, st):
            plo, phi = st
            mid = (plo + phi) // 2
            c = count(lambda s, r0: (s == thr) & (kid0 + r0 <= mid))
            ok = c >= need
            return jnp.where(ok, plo, mid + 1), jnp.where(ok, mid, phi)

        plo0 = jnp.zeros((1, tq), jnp.int32)
        phi0 = jnp.full((1, tq), seq - 1, jnp.int32)
        last, _ = lax.fori_loop(0, max(1, (seq - 1).bit_length()), pos_body, (plo0, phi0))

        def drop_body(j, carry):
            r0 = pl.multiple_of(j * IDX_TK, IDX_TK)
            s = sc_ref[pl.ds(r0, IDX_TK), :]
            sc_ref[pl.ds(r0, IDX_TK), :] = jnp.where((s == thr) & (kid0 + r0 > last), -jnp.inf, s)
            return carry

        lax.fori_loop(0, n_idx, drop_body, 0)

    def set_shift_rows(neg_shift):
        for h in range(H):
            blk = jnp.where(eye_b, neg_shift[:, h * tq:(h + 1) * tq], 0.0)
            qs_ref[KV_RANK:, h * tq:(h + 1) * tq] = blk.astype(BF16)

    def qk(ckv_t, keep):
        lhs = jnp.concatenate([ckv_t, jnp.where(keep, 1.0, MASK_BIG).astype(BF16)], axis=1)
        return jnp.dot(lhs, qs_ref[...], preferred_element_type=F32)

    near0 = jnp.maximum(row0 - tq, 0)
    n_far = (near0 + ATT_TK - 1) // ATT_TK
    fkid0 = lax.broadcasted_iota(jnp.int32, (ATT_TK, tq), 0)
    n0 = pl.multiple_of(near0, tq)
    first = (i == 0).astype(jnp.int32)
    nkid = n0 + lax.broadcasted_iota(jnp.int32, (NEAR_W, tq), 0)
    nqid = row0 + lax.broadcasted_iota(jnp.int32, (NEAR_W, tq), 1)

    def far_tile(j):
        r0 = pl.multiple_of(j * ATT_TK, ATT_TK)
        keep = (sc_ref[pl.ds(r0, ATT_TK), :] >= thr) & (fkid0 + r0 < near0)
        return ckv_ref[pl.ds(r0, ATT_TK), :], ckvt_ref[:, pl.ds(r0, ATT_TK)], keep

    def near_tile():
        keep = (sc_ref[pl.ds(n0, NEAR_W), :] >= thr) & (nkid <= nqid)
        return ckv_ref[pl.ds(n0, NEAR_W), :], ckvt_ref[:, pl.ds(n0, NEAR_W)], keep

    own = ckvt_ref[:, pl.ds(pl.multiple_of(row0, tq), tq)].astype(F32)
    shift = jnp.concatenate(
        [jnp.sum(qs_ref[0:KV_RANK, h * tq:(h + 1) * tq].astype(F32) * own, axis=0, keepdims=True)
         for h in range(H)], axis=1)
    shift = jnp.clip(shift, 1.0, SHIFT_MAX).astype(BF16).astype(F32)
    set_shift_rows(-shift)

    ckv_n, ckvt_n, keep_n = near_tile()
    p = jnp.exp2(qk(ckv_n, keep_n) + tbl_ref[first])
    l_near = jnp.sum(p, axis=0, keepdims=True)
    acc_ref[...] = jnp.dot(ckvt_n, p.astype(BF16), preferred_element_type=F32)

    def far_group(first, unroll, l_f):
        tiles = [far_tile(first + u) for u in range(unroll)]
        ps = [jnp.exp2(qk(ckv_t, keep)) for ckv_t, _, keep in tiles]
        pv = jnp.dot(tiles[0][1], ps[0].astype(BF16), preferred_element_type=F32)
        for u in range(1, unroll):
            pv = pv + jnp.dot(tiles[u][1], ps[u].astype(BF16), preferred_element_type=F32)
        acc_ref[...] += pv
        for p in ps:
            l_f = l_f + jnp.sum(p, axis=0, keepdims=True)
        return l_f

    l_f = l_near
    done = 0
    for unroll in FAR_UNROLLS:
        n_grp = (n_far - done) // unroll
        l_f = lax.fori_loop(0, n_grp, lambda g, l, done=done, unroll=unroll:
                            far_group(done + g * unroll, unroll, l), l_f)
        done = done + n_grp * unroll

    l_tot = l_f
    acc_ref[...] = acc_ref[...] * (1.0 / l_tot)
    in_range = (l_tot > L_TOT_MIN) & (l_tot < L_TOT_MAX)
    n_bad = jnp.sum(jnp.where(in_range, 0.0, 1.0))

    @pl.when(n_bad > 0.0)
    def _():
        set_shift_rows(jnp.full((1, H * tq), -1.0, F32))
        m_ref[...] = jnp.full(m_ref.shape, M_INIT, F32)
        l_ref[...] = jnp.zeros(l_ref.shape, F32)
        acc_ref[...] = jnp.zeros(acc_ref.shape, F32)

        def attend(ckv_t, ckvt_t, keep, bias):
            lg = qk(ckv_t, keep)
            if bias is not None:
                lg = lg + bias
            m_prev = m_ref[...]
            m_new = jnp.maximum(m_prev, jnp.max(lg, axis=0, keepdims=True))
            alpha = jnp.exp2(m_prev - m_new)
            p = jnp.exp2(lg - m_new)
            l_ref[...] = alpha * l_ref[...] + jnp.sum(p, axis=0, keepdims=True)
            m_ref[...] = m_new
            pv = jnp.dot(ckvt_t, p.astype(BF16), preferred_element_type=F32)
            acc_ref[...] = alpha * acc_ref[...] + pv

        def slow_body(j, carry):
            attend(*far_tile(j), None)
            return carry

        lax.fori_loop(0, n_far, slow_body, 0)
        attend(*near_tile(), tbl_ref[first])
        acc_ref[...] = acc_ref[...] * (1.0 / l_ref[...])

    ot = acc_ref[...].astype(BF16)
    o_st = jnp.concatenate([ot[:, h * tq:(h + 1) * tq] for h in range(H)], axis=0)
    yt = jnp.dot(wuvt_ref[...], o_st, preferred_element_type=F32)
    o_ref[...] = yt.T.astype(BF16)


def _dsa(qlt, iqt, iwt, ik, ckv, ckvt, tbl, wuvt, layer, B, S):
    tq = Q_BLOCK
    nb = S // tq
    RL = A_HEADS * KV_RANK
    QI = IDX_HEADS * IDX_DIM
    k_sel = min(TOPK_MAX, S // 4)
    assert IDX_TK % k_sel == 0 and k_sel % SUBLANES == 0
    return pl.pallas_call(
        functools.partial(_dsa_kernel, k_sel=k_sel, seq=S),
        out_shape=jax.ShapeDtypeStruct((B * S, A_WIDTH), BF16),
        grid=(B, nb),
        in_specs=[
            pl.BlockSpec((None, RL, tq), lambda b, i: (b, 0, i)),
            pl.BlockSpec((None, QI, tq), lambda b, i: (b, 0, i)),
            pl.BlockSpec((None, IDX_HEADS, tq), lambda b, i: (b, 0, i)),
            pl.BlockSpec((None, S, IDX_DIM), lambda b, i: (b, 0, 0)),
            pl.BlockSpec((None, S, KV_RANK), lambda b, i: (b, 0, 0)),
            pl.BlockSpec((None, KV_RANK, S), lambda b, i: (b, 0, 0)),
            pl.BlockSpec((2, NEAR_W, A_HEADS * tq), lambda b, i: (0, 0, 0)),
            pl.BlockSpec((None, A_WIDTH, RL), lambda b, i: (layer, 0, 0)),
        ],
        out_specs=pl.BlockSpec((tq, A_WIDTH), lambda b, i: (b * nb + i, 0)),
        scratch_shapes=[
            pltpu.VMEM((S, tq), F32),
            pltpu.VMEM((S, tq), F32),
            pltpu.VMEM((SORT_N * SUBLANES, tq), F32),
            pltpu.VMEM((1, tq), jnp.int32),
            pltpu.VMEM((1, tq), F32),
            pltpu.VMEM((KV_RANK + tq, A_HEADS * tq), BF16),
            pltpu.VMEM((IDX_DIM, IDX_HEADS * tq), BF16),
            pltpu.VMEM((1, A_HEADS * tq), F32),
            pltpu.VMEM((1, A_HEADS * tq), F32),
            pltpu.VMEM((KV_RANK, A_HEADS * tq), F32),
            pltpu.VMEM((k_sel, tq), F32),
        ],
        compiler_params=_params("parallel", "arbitrary"),
        name="dsa",
    )(qlt, iqt, iwt, ik, ckv, ckvt, tbl, wuvt)


def _memkv_kernel(mem_ref, g_ref, w_ref, kbd_ref, vbd_ref):
    M = mem_ref.shape[0]
    mn = _rms(mem_ref[...], g_ref[...]).astype(BF16)
    kv = jnp.dot(mn, w_ref[...], preferred_element_type=F32)
    kt = kv[:, :C_WIDTH].T
    v = kv[:, C_WIDTH:]
    r = lax.broadcasted_iota(jnp.int32, (C_WIDTH, C_HEADS * M), 0) // C_HEAD_DIM
    c = lax.broadcasted_iota(jnp.int32, (C_WIDTH, C_HEADS * M), 1) // M
    kbd_ref[...] = jnp.where(r == c, jnp.concatenate([kt] * C_HEADS, axis=1), 0.0).astype(BF16)
    r = lax.broadcasted_iota(jnp.int32, (C_HEADS * M, C_WIDTH), 0) // M
    c = lax.broadcasted_iota(jnp.int32, (C_HEADS * M, C_WIDTH), 1) // C_HEAD_DIM
    vbd_ref[...] = jnp.where(r == c, jnp.concatenate([v] * C_HEADS, axis=0), 0.0).astype(BF16)


def _memkv(mem, g, w, layer):
    B, M, D = mem.shape
    return pl.pallas_call(
        _memkv_kernel,
        out_shape=[jax.ShapeDtypeStruct((B, C_WIDTH, C_HEADS * M), BF16),
                   jax.ShapeDtypeStruct((B, C_HEADS * M, C_WIDTH), BF16)],
        grid=(B,),
        in_specs=[pl.BlockSpec((None, M, D), lambda b: (b, 0, 0)),
                  pl.BlockSpec((None, 1, D), lambda b: (layer, 0, 0)),
                  pl.BlockSpec((None, D, 2 * C_WIDTH), lambda b: (layer, 0, 0))],
        out_specs=[pl.BlockSpec((None, C_WIDTH, C_HEADS * M), lambda b: (b, 0, 0)),
                   pl.BlockSpec((None, C_HEADS * M, C_WIDTH), lambda b: (b, 0, 0))],
        compiler_params=_params("parallel"),
        name="memkv",
    )(mem, g, w)


def _mix_kernel(h_ref, g_ref, wg_ref, ya_ref, z_ref, zh_ref, qc_ref, kbd_ref, vbd_ref, wpool_ref, ps_ref,
                wa_ref, wb_ref, wc_ref, wo_ref, o_ref, zx_ref):
    tm, D = h_ref.shape
    i = pl.program_id(1)
    M = kbd_ref.shape[1] // C_HEADS
    h = h_ref[...]
    u = _rms(h, g_ref[...]).astype(BF16)

    z = z_ref[...]
    zx_ref[0:POOL_HALO, :] = jnp.where(i == 0, 0.0, zh_ref[...])
    zx_ref[POOL_HALO:, :] = z
    pos = i * tm + lax.broadcasted_iota(jnp.int32, (tm, 1), 0) + 1
    grp = lax.broadcasted_iota(jnp.int32, (1, POOL_WIDTH), 1) // POOL_GROUP
    win = z
    pooled = jnp.zeros_like(z)
    k = 1
    for gi, w in enumerate(POOL_WINDOWS):
        while k < w:
            win = win + zx_ref[POOL_HALO - k:POOL_HALO - k + tm, :]
            k += 1
        cnt = jnp.minimum(pos, w).astype(F32)
        pooled = jnp.where(grp == gi, win / cnt - z, pooled)
    yb = jnp.dot(pooled.astype(BF16), wpool_ref[...], preferred_element_type=F32) * ps_ref[...]

    lg = jnp.dot(qc_ref[...], kbd_ref[...], preferred_element_type=F32)
    ps = []
    for hc in range(C_HEADS):
        seg = lg[:, hc * M:(hc + 1) * M]
        e = jnp.exp(seg - jnp.max(seg, axis=-1, keepdims=True))
        ps.append(e / jnp.sum(e, axis=-1, keepdims=True))
    p = jnp.concatenate(ps, axis=1).astype(BF16)
    yc = jnp.dot(p, vbd_ref[...], preferred_element_type=F32)

    def gate(n):
        pre = jnp.dot(u, wg_ref[:, n * D:(n + 1) * D], preferred_element_type=F32)
        return jax.nn.sigmoid(pre)

    merged = gate(0) * jnp.dot(ya_ref[...], wa_ref[...], preferred_element_type=F32)
    merged += gate(1) * jnp.dot(yb.astype(BF16), wb_ref[...], preferred_element_type=F32)
    merged += gate(2) * jnp.dot(yc.astype(BF16), wc_ref[...], preferred_element_type=F32)
    o_ref[...] = h + jnp.dot(merged.astype(BF16), wo_ref[...], preferred_element_type=F32)


def _mix(h, g, ya, z, qc, kbd, vbd, wp, layer, B, S):
    T, D = h.shape
    tm = min(MIX_TM, S)
    nt = S // tm
    hb = tm // POOL_HALO

    def wspec(arr):
        return pl.BlockSpec((None,) + arr.shape[1:], lambda b, i: (layer,) + (0,) * (arr.ndim - 1))

    tok = lambda w: pl.BlockSpec((tm, w), lambda b, i: (b * nt + i, 0))
    names = ("wpool", "ps", "wa", "wb", "wc", "wo")
    return pl.pallas_call(
        _mix_kernel,
        out_shape=jax.ShapeDtypeStruct((T, D), F32),
        grid=(B, nt),
        in_specs=[tok(D), wspec(g), wspec(wp["wg"]), tok(A_WIDTH), tok(POOL_WIDTH),
                  pl.BlockSpec((POOL_HALO, POOL_WIDTH), lambda b, i: (jnp.maximum((b * nt + i) * hb - 1, 0), 0)),
                  tok(C_WIDTH),
                  pl.BlockSpec((None,) + kbd.shape[1:], lambda b, i: (b, 0, 0)),
                  pl.BlockSpec((None,) + vbd.shape[1:], lambda b, i: (b, 0, 0))]
                 + [wspec(wp[k]) for k in names],
        out_specs=tok(D),
        scratch_shapes=[pltpu.VMEM((tm + POOL_HALO, POOL_WIDTH), F32)],
        compiler_params=_params("parallel", "parallel"),
        name="mix",
    )(h, g, wp["wg"], ya, z, z, qc, kbd, vbd, *[wp[k] for k in names])


def _block_diag(blocks):
    L, G, r, c = blocks.shape
    eye = jnp.eye(G, dtype=blocks.dtype)
    return jnp.einsum("lgrc,gh->lgrhc", blocks, eye).reshape(L, G * r, G * c)


def _prep_weights(w_in, kv_norm, w_uk, w_uv, w_pool, pool_scale, w_branch_a, w_branch_b, w_branch_c, w_out):
    L, D, _ = w_in.shape
    sizes = (A_WIDTH, KV_RANK, IDX_HEADS * IDX_DIM, IDX_DIM, IDX_HEADS, POOL_WIDTH, C_WIDTH, N_BRANCH * D)
    offs = np.concatenate([[0], np.cumsum(sizes)])
    wq, wckv, wiq, wik, wiw, wz, wqc, wg = [w_in[:, :, offs[n]:offs[n + 1]] for n in range(len(sizes))]
    proj = dict(
        wq=wq.astype(BF16),
        wukt=_block_diag(jnp.transpose(w_uk, (0, 2, 1, 3))).astype(BF16),
        wckv=wckv.astype(BF16),
        kvg=kv_norm[:, None, :],
        wiqt=jnp.transpose(wiq, (0, 2, 1)).astype(BF16),
        wik=jnp.pad(wik, ((0, 0), (0, 0), (0, LANES - IDX_DIM))).astype(BF16),
        wiwt=jnp.pad(jnp.transpose(wiw, (0, 2, 1)), ((0, 0), (0, BF16_ROWS - IDX_HEADS), (0, 0))).astype(BF16),
        wz=wz.astype(BF16),
        wqc=wqc.astype(BF16),
    )
    wuvt = _block_diag(jnp.transpose(w_uv, (0, 2, 3, 1))).astype(BF16)
    mix = dict(
        wg=wg.astype(BF16),
        wpool=_block_diag(w_pool).astype(BF16),
        ps=pool_scale[:, None, :],
        wa=w_branch_a.astype(BF16), wb=w_branch_b.astype(BF16), wc=w_branch_c.astype(BF16),
        wo=w_out.astype(BF16),
    )
    return proj, wuvt, mix


def kernel(x, mem, rel_bias, ffn1_norm, ffn1_w_in, ffn1_w_out, mix_norm, w_in, kv_norm, w_uk, w_uv, w_pool,
           pool_scale, mem_norm, w_mem_kv, w_branch_a, w_branch_b, w_branch_c, w_out, ffn2_norm, ffn2_w_in,
           ffn2_w_out, final_norm):
    B, S, D = x.shape
    L = w_in.shape[0]
    assert S % Q_BLOCK == 0 and S % IDX_TK == 0 and S % ATT_TK == 0 and S >= NEAR_W

    proj_w, wuvt, mix_w = _prep_weights(w_in, kv_norm, w_uk, w_uv, w_pool, pool_scale,
                                        w_branch_a, w_branch_b, w_branch_c, w_out)
    f1_in, f1_out = ffn1_w_in.astype(BF16), ffn1_w_out.astype(BF16)
    f2_in, f2_out = ffn2_w_in.astype(BF16), ffn2_w_out.astype(BF16)
    f1_g, f2_g, mx_g, mm_g = (a[:, None, :] for a in (ffn1_norm, ffn2_norm, mix_norm, mem_norm))
    fg = final_norm[None, :]
    wmem = w_mem_kv.astype(BF16)
    tbl = _bias_tables(rel_bias)

    h = x.reshape(B * S, D)
    for l in range(L):
        h = _ffn(h, f1_g, f1_in, f1_out, fg, l, final=False)
        qlt, ckv, ckvt, iqt, ik, iwt, z, qc = _proj(h, mx_g, proj_w, l, B, S)
        ya = _dsa(qlt, iqt, iwt, ik.reshape(B, S, IDX_DIM), ckv.reshape(B, S, KV_RANK), ckvt, tbl, wuvt, l, B, S)
        kbd, vbd = _memkv(mem, mm_g, wmem, l)
        h = _mix(h, mx_g, ya, z, qc, kbd, vbd, mix_w, l, B, S)
        h = _ffn(h, f2_g, f2_in, f2_out, fg, l, final=(l == L - 1))
    return h.reshape(B, S, D)
```
